```python
import math
import jax, jax.numpy as jnp
from jax import lax
import numpy as np

D_MODEL = 1024
BATCH = 4
SEQ = 8192
DEPTH = 4

EPS = 1e-6
CONV_CH = 512
CONV_K = 31
N_HEADS = 8
HEAD_DIM = 64
ATT_W = N_HEADS * HEAD_DIM
N_IDX_HEADS = 4
IDX_DIM = 64
TOPK_MAX = 256
Q_BLOCK = 128
NUM_BUCKETS = 32
MAX_DISTANCE = 128
SSM_HEADS = 8
SSM_HEAD_DIM = 64
SSM_INNER = SSM_HEADS * SSM_HEAD_DIM
SSM_GROUPS = 2
SSM_STATE = 128
SSM_CONV_K = 4
SSM_XBC = SSM_INNER + 2 * SSM_GROUPS * SSM_STATE
CHUNK = 128
D_MIX = CONV_CH + ATT_W + SSM_INNER
COL_SIZES = [CONV_CH, CONV_CH,
             ATT_W, ATT_W, ATT_W,
             N_IDX_HEADS * IDX_DIM, IDX_DIM, N_IDX_HEADS,
             SSM_INNER, SSM_XBC, SSM_HEADS]
SPLITS = [int(v) for v in np.cumsum(COL_SIZES)[:-1]]
D_IN = int(sum(COL_SIZES))
D_FF = 2816
FFN_CONV_K = 3

kernel_name = "hybrid_conv_dsa_ssd_block"


def rms_norm(x, g):
    xf = x.astype(jnp.float32)
    y = xf * lax.rsqrt(jnp.mean(xf * xf, axis=-1, keepdims=True) + EPS)
    return (y * g.astype(jnp.float32)).astype(x.dtype)


def layer_norm(x, g, b):
    xf = x.astype(jnp.float32)
    mu = jnp.mean(xf, axis=-1, keepdims=True)
    var = jnp.mean(jnp.square(xf - mu), axis=-1, keepdims=True)
    y = (xf - mu) * lax.rsqrt(var + EPS)
    return (y * g.astype(jnp.float32) + b.astype(jnp.float32)).astype(x.dtype)


def causal_dwconv(u, w, b):
    k_w, c = w.shape
    y = lax.conv_general_dilated(
        u, w[:, None, :].astype(u.dtype), window_strides=(1,), padding=[(k_w - 1, 0)],
        dimension_numbers=("NWC", "WIO", "NWC"), feature_group_count=c)
    return y + b.astype(u.dtype)


def t5_bucket(rel):
    n = jnp.maximum(rel, 0)
    max_exact = NUM_BUCKETS // 2
    nf = jnp.maximum(n, 1).astype(jnp.float32)
    large = max_exact + (jnp.log(nf / max_exact) / math.log(MAX_DISTANCE / max_exact)
                         * (NUM_BUCKETS - max_exact)).astype(jnp.int32)
    large = jnp.minimum(large, NUM_BUCKETS - 1)
    return jnp.where(n < max_exact, n, large)


def sparse_attention(q, k, v, iq, ik, iw, rel_bias):
    bsz, s_len = q.shape[0], q.shape[1]
    topk = min(TOPK_MAX, s_len // 4)
    n_blocks = s_len // Q_BLOCK
    s_pos = jnp.arange(s_len)

    def block(i):
        q0 = i * Q_BLOCK
        qb = lax.dynamic_slice_in_dim(q, q0, Q_BLOCK, 1)
        iqb = lax.dynamic_slice_in_dim(iq, q0, Q_BLOCK, 1)
        iwb = lax.dynamic_slice_in_dim(iw, q0, Q_BLOCK, 1)
        t_pos = q0 + jnp.arange(Q_BLOCK)
        dots = jnp.einsum("bqhd,bsd->bqsh", iqb, ik) * (IDX_DIM ** -0.5)
        score = jnp.einsum("bqsh,bqh->bqs", jax.nn.relu(dots), iwb) * (N_IDX_HEADS ** -0.5)
        causal = s_pos[None, :] <= t_pos[:, None]
        score = jnp.where(causal[None], score.astype(jnp.float32), -jnp.inf)
        _, idx = lax.top_k(score, topk)
        kg = jax.vmap(lambda kk, ii: kk[ii])(k, idx)
        vg = jax.vmap(lambda vv, ii: vv[ii])(v, idx)
        logits = jnp.einsum("bqhd,bqkhd->bqhk", qb, kg).astype(jnp.float32) * (HEAD_DIM ** -0.5)
        rel = t_pos[None, :, None] - idx
        bias = rel_bias[t5_bucket(rel)].astype(jnp.float32)
        logits = logits + jnp.transpose(bias, (0, 1, 3, 2))
        logits = jnp.where((rel >= 0)[:, :, None, :], logits, -jnp.inf)
        p = jax.nn.softmax(logits, axis=-1).astype(v.dtype)
        return jnp.einsum("bqhk,bqkhd->bqhd", p, vg)

    out = lax.map(block, jnp.arange(n_blocks))
    return jnp.transpose(out, (1, 0, 2, 3, 4)).reshape(bsz, s_len, ATT_W)


def ssd_scan(xh, dt, a, bm, cm):
    bsz, s_len, n_h, p_dim = xh.shape
    n_c = s_len // CHUNK
    rep = n_h // SSM_GROUPS
    bh = jnp.repeat(bm, rep, axis=2).reshape(bsz, n_c, CHUNK, n_h, SSM_STATE)
    ch = jnp.repeat(cm, rep, axis=2).reshape(bsz, n_c, CHUNK, n_h, SSM_STATE)
    xd = (xh.astype(jnp.float32) * dt[..., None]).reshape(bsz, n_c, CHUNK, n_h, p_dim)
    cs = jnp.cumsum((dt * a).reshape(bsz, n_c, CHUNK, n_h), axis=2)
    seg = cs[:, :, :, None, :] - cs[:, :, None, :, :]
    tri = jnp.tril(jnp.ones((CHUNK, CHUNK), dtype=bool))
    decay_ls = jnp.exp(jnp.where(tri[None, None, :, :, None], seg, -jnp.inf))
    scores = jnp.einsum("bclhn,bcshn->bclsh", ch, bh).astype(jnp.float32) * decay_ls
    y_diag = jnp.einsum("bclsh,bcshp->bclhp", scores, xd)
    decay_end = jnp.exp(cs[:, :, -1:, :] - cs)
    states = jnp.einsum("bclhn,bclh,bclhp->bchpn", bh.astype(jnp.float32), decay_end, xd)
    chunk_decay = jnp.exp(cs[:, :, -1, :])

    def step(h, inp):
        st, dec = inp
        return h * dec[:, :, None, None] + st, h

    h0 = jnp.zeros((bsz, n_h, p_dim, SSM_STATE), jnp.float32)
    _, prev = lax.scan(step, h0, (jnp.moveaxis(states, 1, 0), jnp.moveaxis(chunk_decay, 1, 0)))
    prev = jnp.moveaxis(prev, 0, 1)
    y_off = jnp.einsum("bclhn,bchpn,bclh->bclhp", ch.astype(jnp.float32), prev, jnp.exp(cs))
    return (y_diag + y_off).reshape(bsz, s_len, n_h, p_dim)


def setup_inputs(seed: int = 0) -> dict:
    key = jax.random.key(seed)
    ks = jax.random.split(key, 24)
    f32 = jnp.float32
    nrm = lambda k, shape, scale: jax.random.normal(k, shape, f32) * scale
    gain = lambda k, shape: 1.0 + 0.05 * jax.random.normal(k, shape, f32)
    dt0 = jnp.exp(jax.random.uniform(ks[14], (DEPTH, SSM_HEADS), f32, math.log(1e-3), math.log(1e-1)))
    return {
        "x": jax.random.normal(ks[0], (BATCH, SEQ, D_MODEL), f32),
        "rel_bias": nrm(ks[1], (NUM_BUCKETS, N_HEADS), 0.1),
        "norm_mix_g": gain(ks[2], (DEPTH, D_MODEL)),
        "w_in": nrm(ks[3], (DEPTH, D_MODEL, D_IN), D_MODEL ** -0.5),
        "conv_dw_w": nrm(ks[4], (DEPTH, CONV_K, CONV_CH), CONV_K ** -0.5),
        "conv_dw_b": nrm(ks[5], (DEPTH, CONV_CH), 0.02),
        "conv_ln_g": gain(ks[6], (DEPTH, CONV_CH)),
        "conv_ln_b": nrm(ks[7], (DEPTH, CONV_CH), 0.02),
        "q_norm_g": gain(ks[8], (DEPTH, HEAD_DIM)),
        "k_norm_g": gain(ks[9], (DEPTH, HEAD_DIM)),
        "ssm_conv_w": nrm(ks[10], (DEPTH, SSM_CONV_K, SSM_XBC), SSM_CONV_K ** -0.5),
        "ssm_conv_b": nrm(ks[11], (DEPTH, SSM_XBC), 0.02),
        "dt_bias": dt0 + jnp.log(-jnp.expm1(-dt0)),
        "a_log": jnp.log(jax.random.uniform(ks[12], (DEPTH, SSM_HEADS), f32, 1.0, 16.0)),
        "d_skip": gain(ks[13], (DEPTH, SSM_HEADS)),
        "ssm_norm_g": gain(ks[15], (DEPTH, SSM_INNER)),
        "w_out": nrm(ks[16], (DEPTH, D_MIX, D_MODEL), D_MIX ** -0.5),
        "norm_ffn_g": gain(ks[17], (DEPTH, D_MODEL)),
        "w_up": nrm(ks[18], (DEPTH, D_MODEL, 2 * D_FF), D_MODEL ** -0.5),
        "ffn_conv_w": nrm(ks[19], (DEPTH, FFN_CONV_K, 2 * D_FF), FFN_CONV_K ** -0.5),
        "ffn_conv_b": nrm(ks[20], (DEPTH, 2 * D_FF), 0.02),
        "w_down": nrm(ks[21], (DEPTH, D_FF, D_MODEL), D_FF ** -0.5),
    }


def reference(x, rel_bias, norm_mix_g, w_in, conv_dw_w, conv_dw_b, conv_ln_g, conv_ln_b,
              q_norm_g, k_norm_g, ssm_conv_w, ssm_conv_b, dt_bias, a_log, d_skip, ssm_norm_g,
              w_out, norm_ffn_g, w_up, ffn_conv_w, ffn_conv_b, w_down):
    bsz, s_len, _ = x.shape
    for i in range(DEPTH):
        h = rms_norm(x, norm_mix_g[i])
        proj = h @ w_in[i]
        ca, cg, q, k, v, iq, ik, iw, z, xbc, dt = jnp.split(proj, SPLITS, axis=-1)

        u = ca * jax.nn.sigmoid(cg)
        u = causal_dwconv(u, conv_dw_w[i], conv_dw_b[i])
        u = jax.nn.silu(layer_norm(u, conv_ln_g[i], conv_ln_b[i]))

        qh = rms_norm(q.reshape(bsz, s_len, N_HEADS, HEAD_DIM), q_norm_g[i])
        kh = rms_norm(k.reshape(bsz, s_len, N_HEADS, HEAD_DIM), k_norm_g[i])
        vh = v.reshape(bsz, s_len, N_HEADS, HEAD_DIM)
        att = sparse_attention(qh, kh, vh,
                               iq.reshape(bsz, s_len, N_IDX_HEADS, IDX_DIM), ik, iw, rel_bias)

        xbc = jax.nn.silu(causal_dwconv(xbc, ssm_conv_w[i], ssm_conv_b[i]))
        xs, bm, cm = jnp.split(xbc, [SSM_INNER, SSM_INNER + SSM_GROUPS * SSM_STATE], axis=-1)
        xs_h = xs.reshape(bsz, s_len, SSM_HEADS, SSM_HEAD_DIM)
        dtf = jax.nn.softplus(dt.astype(jnp.float32) + dt_bias[i].astype(jnp.float32))
        a = -jnp.exp(a_log[i].astype(jnp.float32))
        y = ssd_scan(xs_h, dtf, a,
                     bm.reshape(bsz, s_len, SSM_GROUPS, SSM_STATE),
                     cm.reshape(bsz, s_len, SSM_GROUPS, SSM_STATE))
        y = (y + d_skip[i].astype(jnp.float32)[:, None] * xs_h.astype(jnp.float32)).astype(x.dtype)
        y = rms_norm(y.reshape(bsz, s_len, SSM_INNER) * jax.nn.silu(z), ssm_norm_g[i])

        x = x + jnp.concatenate([u, att, y], axis=-1) @ w_out[i]

        h = rms_norm(x, norm_ffn_g[i])
        f = causal_dwconv(h @ w_up[i], ffn_conv_w[i], ffn_conv_b[i])
        fa, fg = jnp.split(f, 2, axis=-1)
        x = x + (jax.nn.silu(fg) * fa) @ w_down[i]
    return x
```

```python
import functools
import math

import jax
import jax.numpy as jnp
import numpy as np
from jax import lax
from jax.experimental import pallas as pl
from jax.experimental.pallas import tpu as pltpu

F32 = jnp.float32
BF16 = jnp.bfloat16

EPS = 1e-6
LANES = 128
CONV_CH = 512
CONV_K = 31
N_HEADS = 8
HEAD_DIM = 64
ATT_W = N_HEADS * HEAD_DIM
N_IDX_HEADS = 4
IDX_DIM = 64
TOPK_MAX = 256
NUM_BUCKETS = 32
MAX_DISTANCE = 128
SSM_HEADS = 8
SSM_HEAD_DIM = 64
SSM_INNER = SSM_HEADS * SSM_HEAD_DIM
SSM_GROUPS = 2
SSM_STATE = 128
SSM_CONV_K = 4
SSM_XBC = SSM_INNER + 2 * SSM_GROUPS * SSM_STATE
CHUNK = 128
D_MIX = CONV_CH + ATT_W + SSM_INNER
D_FF = 2816
FFN_CONV_K = 3

MISC_IW = 0
MISC_DT = 8
HALO = 8
CONV_HALO = 32
QB = 128
KC = 512
NEG = -1e30
VMEM_LIMIT = 56 * 1024 * 1024

NT_DIMS = (((1,), (1,)), ((), ()))


def _dot(a, b):
    return jnp.dot(a, b, preferred_element_type=F32)


def _dot_nt(a, b):
    return lax.dot_general(a, b, NT_DIMS, preferred_element_type=F32)


def _dot_split(x, m_bf16):
    hi = x.astype(BF16)
    r1 = x - hi.astype(F32)
    mid = r1.astype(BF16)
    lo = (r1 - mid.astype(F32)).astype(BF16)
    return _dot(hi, m_bf16) + _dot(mid, m_bf16) + _dot(lo, m_bf16)


def _silu(x):
    return x * jax.nn.sigmoid(x)


def _const_spec(shape):
    n = len(shape)
    return pl.BlockSpec(shape, lambda *_: (0,) * n)


def _in_proj_kernel(x_ref, g_ref, w_ref, wm_ref, gq_ref, gk_ref, seg_ref,
                    cacg_ref, q_ref, k_ref, v_ref, iq_ref, ik_ref, z_ref, xbc_ref, misc_ref):
    x = x_ref[...]
    h = x * lax.rsqrt(jnp.mean(x * x, axis=-1, keepdims=True) + EPS) * g_ref[...]
    hb = h.astype(BF16)

    def proj(c0, width):
        return _dot(hb, w_ref[:, c0:c0 + width])

    cacg_ref[...] = proj(0, 1024).astype(BF16)

    def qk_norm(c0, gain_ref, scale):
        t = proj(c0, ATT_W)
        ms = _dot_split(t * t, seg_ref[...])
        return (t * lax.rsqrt(ms + EPS) * (gain_ref[...] * scale)).astype(BF16)

    q_ref[...] = qk_norm(1024, gq_ref, HEAD_DIM ** -0.5)
    k_ref[...] = qk_norm(1536, gk_ref, 1.0)
    v_ref[...] = proj(2048, ATT_W).astype(BF16)
    iq_ref[...] = proj(2560, N_IDX_HEADS * IDX_DIM).astype(BF16)
    ik_ref[...] = proj(2816, LANES).astype(BF16)
    z_ref[...] = proj(2944, SSM_INNER).astype(BF16)
    xbc_ref[...] = proj(3456, SSM_XBC).astype(BF16)
    misc_ref[...] = _dot(hb, wm_ref[...])


W_MAIN_COLS = 3456 + SSM_XBC


def _in_proj(x2, g, w_main, w_misc, gq, gk, seg, tm):
    t, d = x2.shape
    outs = [(1024, BF16), (ATT_W, BF16), (ATT_W, BF16), (ATT_W, BF16),
            (N_IDX_HEADS * IDX_DIM, BF16), (LANES, BF16), (SSM_INNER, BF16),
            (SSM_XBC, BF16), (LANES, F32)]
    return pl.pallas_call(
        _in_proj_kernel,
        grid=(t // tm,),
        in_specs=[pl.BlockSpec((tm, d), lambda i: (i, 0)),
                  _const_spec((1, d)),
                  _const_spec((d, W_MAIN_COLS)),
                  _const_spec((d, LANES)),
                  _const_spec((1, ATT_W)), _const_spec((1, ATT_W)),
                  _const_spec((ATT_W, ATT_W))],
        out_specs=[pl.BlockSpec((tm, w), lambda i: (i, 0)) for w, _ in outs],
        out_shape=[jax.ShapeDtypeStruct((t, w), dt) for w, dt in outs],
        compiler_params=pltpu.CompilerParams(
            dimension_semantics=("arbitrary",), vmem_limit_bytes=VMEM_LIMIT),
        name="in_proj",
    )(x2, g, w_main, w_misc, gq, gk, seg)


CONV_ROWS = 64


def _conv_kernel(cacg_ref, w_ref, b_ref, g_ref, beta_ref, o_ref, ext_ref):
    ts = o_ref.shape[1]
    j = pl.program_id(1)

    @pl.when(j == 0)
    def _():
        ext_ref[0:CONV_HALO, :] = jnp.zeros((CONV_HALO, CONV_CH), F32)

    @pl.when(j > 0)
    def _():
        ext_ref[0:CONV_HALO, :] = ext_ref[ts:ts + CONV_HALO, :]

    ca = cacg_ref[0, :, 0:CONV_CH].astype(F32)
    cg = cacg_ref[0, :, CONV_CH:2 * CONV_CH].astype(F32)
    ext_ref[CONV_HALO:, :] = ca * jax.nn.sigmoid(cg)

    w = w_ref[...]
    off = CONV_HALO - (CONV_K - 1)
    for r in range(ts // CONV_ROWS):
        base = r * CONV_ROWS
        acc = jnp.broadcast_to(b_ref[...], (CONV_ROWS, CONV_CH))
        for tap in range(CONV_K):
            acc = acc + w[tap:tap + 1, :] * ext_ref[base + off + tap:base + off + tap + CONV_ROWS, :]
        mu = jnp.mean(acc, axis=-1, keepdims=True)
        cen = acc - mu
        var = jnp.mean(cen * cen, axis=-1, keepdims=True)
        yn = cen * lax.rsqrt(var + EPS) * g_ref[...] + beta_ref[...]
        o_ref[0, base:base + CONV_ROWS, :] = _silu(yn).astype(BF16)


def _conv_module(cacg, w, b, g, beta, ts):
    bsz, s, _ = cacg.shape
    return pl.pallas_call(
        _conv_kernel,
        grid=(bsz, s // ts),
        in_specs=[pl.BlockSpec((1, ts, 2 * CONV_CH), lambda bi, j: (bi, j, 0)),
                  _const_spec((CONV_HALO, CONV_CH)),
                  _const_spec((1, CONV_CH)), _const_spec((1, CONV_CH)), _const_spec((1, CONV_CH))],
        out_specs=pl.BlockSpec((1, ts, CONV_CH), lambda bi, j: (bi, j, 0)),
        out_shape=jax.ShapeDtypeStruct((bsz, s, CONV_CH), BF16),
        scratch_shapes=[pltpu.VMEM((ts + CONV_HALO, CONV_CH), F32)],
        compiler_params=pltpu.CompilerParams(
            dimension_semantics=("arbitrary", "arbitrary"), vmem_limit_bytes=VMEM_LIMIT),
        name="conv_module",
    )(cacg, w, b, g, beta)


def _key_to_float(key):
    bits = key ^ ((key >> 31) & jnp.int32(0x7FFFFFFF))
    return lax.bitcast_convert_type(bits, F32)


KEY_NEG_INF = int(np.int32(np.uint32(0xFF800000)) ^ np.int32(0x7FFFFFFF))
KEY_POS_INF_NEXT = 0x7F800000 + 1


def _dsa_kernel(far_ref, q_ref, k_ref, v_ref, iq_ref, ik_ref, misc_ref, nb_ref, o_ref,
                sc_ref, qm_ref, m_ref, l_ref, acc_ref, thr_ref, pos_ref, *, topk):
    i = pl.program_id(1)
    s_len = k_ref.shape[1]
    q0 = i * QB
    n_blk = i + 1
    lane = lax.broadcasted_iota(jnp.int32, (QB, LANES), 1)
    row = lax.broadcasted_iota(jnp.int32, (QB, LANES), 0)
    left = lane < HEAD_DIM
    qpos = q0 + row

    iq = iq_ref[0]
    misc = misc_ref[0]
    iqm, iw = [], []
    for h in range(N_IDX_HEADS):
        pair = iq[:, (h // 2) * LANES:(h // 2 + 1) * LANES]
        iqm.append(jnp.where(left if h % 2 == 0 else ~left, pair, jnp.zeros_like(pair)))
        iw.append(misc[:, MISC_IW + h:MISC_IW + h + 1] * (IDX_DIM ** -0.5 * N_IDX_HEADS ** -0.5))

    def score_chunk(c, carry):
        k0 = pl.multiple_of(c * KC, KC)
        ikc = ik_ref[0, pl.ds(k0, KC), :]
        s = jnp.zeros((QB, KC), F32)
        for h in range(N_IDX_HEADS):
            s = s + jnp.maximum(_dot_nt(iqm[h], ikc), 0.0) * iw[h]
        for t in range(KC // LANES):
            kpos = k0 + t * LANES + lane
            sc_ref[c * (KC // LANES) + t] = jnp.where(
                kpos <= qpos, s[:, t * LANES:(t + 1) * LANES], -jnp.inf)
        return carry

    lax.fori_loop(0, i // (KC // QB) + 1, score_chunk, 0)

    def count_blocks(pred):
        def body(bk, acc):
            return acc + jnp.where(pred(sc_ref[bk], bk), 1.0, 0.0)
        acc = lax.fori_loop(0, n_blk, body, jnp.zeros((QB, LANES), F32))
        return jnp.sum(acc, axis=-1, keepdims=True)

    def bisect(_, lohi):
        lo, hi = lohi
        mid = (lo >> 1) + (hi >> 1) + (lo & hi & 1)
        thr_b = jnp.broadcast_to(_key_to_float(mid), (QB, LANES))
        ok = count_blocks(lambda s, bk: s >= thr_b) >= topk
        return jnp.where(ok, mid, lo), jnp.where(ok, hi, mid)

    lo0 = jnp.full((QB, 1), KEY_NEG_INF, jnp.int32)
    hi0 = jnp.full((QB, 1), KEY_POS_INF_NEXT, jnp.int32)
    lo, _ = lax.fori_loop(0, 32, bisect, (lo0, hi0))
    thr = _key_to_float(lo)
    thr_b = jnp.broadcast_to(thr, (QB, LANES))
    n_gt = count_blocks(lambda s, bk: s > thr_b)
    n_ge = count_blocks(lambda s, bk: s >= thr_b)
    thr_ref[...] = thr_b
    pos_ref[...] = jnp.full((QB, LANES), s_len, jnp.int32)

    @pl.when(jnp.max(n_ge) > topk)
    def _():
        need = topk - n_gt

        def bisect_pos(_, lohi):
            plo, phi = lohi
            mid_b = jnp.broadcast_to((plo + phi) >> 1, (QB, LANES))
            cnt = count_blocks(lambda s, bk: (s == thr_b) & (bk * LANES + lane <= mid_b))
            ok = cnt >= need
            return jnp.where(ok, plo, (plo + phi) >> 1), jnp.where(ok, (plo + phi) >> 1, phi)

        plo0 = jnp.full((QB, 1), -1, jnp.int32)
        phi0 = jnp.full((QB, 1), n_blk * LANES - 1, jnp.int32)
        n_iter = int(math.ceil(math.log2(s_len))) + 1
        _, phi = lax.fori_loop(0, n_iter, bisect_pos, (plo0, phi0))
        pos_ref[...] = jnp.broadcast_to(phi, (QB, LANES))

    q = q_ref[0]
    for h in range(N_HEADS):
        pair = q[:, (h // 2) * LANES:(h // 2 + 1) * LANES]
        qm_ref[h] = jnp.where(left if h % 2 == 0 else ~left, pair, jnp.zeros_like(pair))
    m_ref[...] = jnp.full(m_ref.shape, NEG, F32)
    l_ref[...] = jnp.zeros(l_ref.shape, F32)
    acc_ref[...] = jnp.zeros(acc_ref.shape, F32)

    def mask_tile(bk, limit):
        s = sc_ref[bk]
        t = thr_ref[...]
        kpos = bk * LANES + lane
        sel = (s > t) | ((s == t) & (kpos <= pos_ref[...]))
        sel = sel & (kpos < limit) & (kpos <= qpos)
        return jnp.where(sel, 0.0, NEG)

    def attend(k0, width, maskadd, bias_of_head):
        for pair in range(N_HEADS // 2):
            kp = k_ref[0, pl.ds(k0, width), pair * LANES:(pair + 1) * LANES]
            vp = v_ref[0, pl.ds(k0, width), pair * LANES:(pair + 1) * LANES]
            pv, al = [], []
            for hh in range(2):
                h = 2 * pair + hh
                lg = _dot_nt(qm_ref[h], kp) + bias_of_head(h) + maskadd
                m_old = m_ref[h]
                m_new = jnp.maximum(m_old, jnp.max(lg, axis=-1, keepdims=True))
                alpha = jnp.exp(m_old - m_new)
                p = jnp.exp(lg - m_new)
                l_ref[h] = alpha * l_ref[h] + jnp.sum(p, axis=-1, keepdims=True)
                m_ref[h] = m_new
                pv.append(_dot(p.astype(BF16), vp))
                al.append(jnp.broadcast_to(alpha, (QB, LANES)))
            acc_ref[pair] = (jnp.where(left, al[0], al[1]) * acc_ref[pair]
                             + jnp.where(left, pv[0], pv[1]))

    far_end = jnp.maximum(i - 1, 0) * QB

    def far_chunk(c, carry):
        k0 = pl.multiple_of(c * KC, KC)
        maskadd = jnp.concatenate(
            [mask_tile(c * (KC // LANES) + t, far_end) for t in range(KC // LANES)], axis=-1)
        attend(k0, KC, maskadd, lambda h: far_ref[h])
        return carry

    lax.fori_loop(0, (far_end + KC - 1) // KC, far_chunk, 0)

    @pl.when(i >= 1)
    def _():
        k0 = pl.multiple_of((i - 1) * QB, QB)
        attend(k0, QB, mask_tile(i - 1, s_len), lambda h: nb_ref[0, h])

    attend(pl.multiple_of(q0, QB), QB, mask_tile(i, s_len), lambda h: nb_ref[1, h])

    for pair in range(N_HEADS // 2):
        l_pair = jnp.where(left, jnp.broadcast_to(l_ref[2 * pair], (QB, LANES)),
                           jnp.broadcast_to(l_ref[2 * pair + 1], (QB, LANES)))
        o_ref[0, :, pair * LANES:(pair + 1) * LANES] = (acc_ref[pair] / l_pair).astype(BF16)


def _dsa(far_bias, q, k, v, iq, ik2, misc, near_bias):
    bsz, s, _ = q.shape
    topk = min(TOPK_MAX, s // 4)
    nq = s // QB
    blk = lambda w: pl.BlockSpec((1, QB, w), lambda bi, i: (bi, i, 0))
    full = lambda w: pl.BlockSpec((1, s, w), lambda bi, i: (bi, 0, 0))
    return pl.pallas_call(
        functools.partial(_dsa_kernel, topk=topk),
        grid=(bsz, nq),
        in_specs=[pl.BlockSpec(memory_space=pltpu.SMEM),
                  blk(ATT_W), full(ATT_W), full(ATT_W),
                  blk(N_IDX_HEADS * IDX_DIM), full(LANES), blk(LANES),
                  _const_spec((2, N_HEADS, QB, QB))],
        out_specs=blk(ATT_W),
        out_shape=jax.ShapeDtypeStruct((bsz, s, ATT_W), BF16),
        scratch_shapes=[
            pltpu.VMEM((_round_up(s, KC) // LANES, QB, LANES), F32),
            pltpu.VMEM((N_HEADS, QB, LANES), BF16),
            pltpu.VMEM((N_HEADS, QB, 1), F32),
            pltpu.VMEM((N_HEADS, QB, 1), F32),
            pltpu.VMEM((N_HEADS // 2, QB, LANES), F32),
            pltpu.VMEM((QB, LANES), F32),
            pltpu.VMEM((QB, LANES), jnp.int32),
        ],
        compiler_params=pltpu.CompilerParams(
            dimension_semantics=("arbitrary", "arbitrary"), vmem_limit_bytes=VMEM_LIMIT),
        name="dsa",
    )(far_bias, q, k, v, iq, ik2, misc, near_bias)


def _round_up(a, b):
    return (a + b - 1) // b * b


def _ssd_kernel(xbc_ref, z_ref, misc_ref, cw_ref, cb_ref, dtb_ref, alog_ref, dsk_ref, gn_ref,
                o_ref, ext_ref, st_ref):
    ts = o_ref.shape[1]
    j = pl.program_id(1)

    @pl.when(j == 0)
    def _():
        ext_ref[0:HALO, :] = jnp.zeros((HALO, SSM_XBC), F32)
        st_ref[...] = jnp.zeros(st_ref.shape, F32)

    @pl.when(j > 0)
    def _():
        ext_ref[0:HALO, :] = ext_ref[ts:ts + HALO, :]

    ext_ref[HALO:, :] = xbc_ref[0].astype(F32)

    lane = lax.broadcasted_iota(jnp.int32, (CHUNK, LANES), 1)
    row = lax.broadcasted_iota(jnp.int32, (CHUNK, LANES), 0)
    tril = row >= lane
    tril_b = jnp.where(tril, 1.0, 0.0).astype(BF16)
    is_dt = (lane >= MISC_DT) & (lane < MISC_DT + SSM_HEADS)
    e_row = lax.broadcasted_iota(jnp.int32, (LANES, SSM_INNER), 0)
    e_col = lax.broadcasted_iota(jnp.int32, (LANES, SSM_INNER), 1)
    expand = jnp.where(e_row - MISC_DT == e_col // SSM_HEAD_DIM, 1.0, 0.0).astype(BF16)
    left = lane < SSM_HEAD_DIM
    lane1 = lax.broadcasted_iota(jnp.int32, (1, LANES), 1)
    a_row = jnp.where((lane1 >= MISC_DT) & (lane1 < MISC_DT + SSM_HEADS), -jnp.exp(alog_ref[...]), 0.0)
    cw = cw_ref[...]
    off = HALO - (SSM_CONV_K - 1)
    gw = SSM_GROUPS * SSM_STATE

    for c in range(ts // CHUNK):
        r0 = c * CHUNK
        conv = jnp.broadcast_to(cb_ref[...], (CHUNK, SSM_XBC))
        for tap in range(SSM_CONV_K):
            conv = conv + cw[tap:tap + 1, :] * ext_ref[r0 + off + tap:r0 + off + tap + CHUNK, :]
        xc = _silu(conv)
        xs = xc[:, 0:SSM_INNER]

        dt = jnp.where(is_dt, jax.nn.softplus(misc_ref[0, r0:r0 + CHUNK, :] + dtb_ref[...]), 0.0)
        cs = _cumsum_rows(tril_b, dt * a_row)
        cs_last = cs[CHUNK - 1:CHUNK, :]
        dt_e = _dot_split(dt, expand)
        dec_e = _dot_split(jnp.exp(cs_last - cs), expand)
        ecs_e = _dot_split(jnp.exp(cs), expand)
        cs_t = cs.T

        xd = xs * dt_e
        xd_b = xd.astype(BF16)
        xdd_b = (xd * dec_e).astype(BF16)
        y = dsk_ref[...] * xs
        st_old = st_ref[...]
        st_b = st_old.astype(BF16)
        new_states = []
        y_parts = []
        for g in range(SSM_GROUPS):
            bm = xc[:, SSM_INNER + g * SSM_STATE:SSM_INNER + (g + 1) * SSM_STATE]
            cm_b = xc[:, SSM_INNER + gw + g * SSM_STATE:SSM_INNER + gw + (g + 1) * SSM_STATE].astype(BF16)
            cb = _dot_nt(cm_b, bm.astype(BF16))
            hpg = SSM_HEADS // SSM_GROUPS
            gl = slice(g * hpg * SSM_HEAD_DIM, (g + 1) * hpg * SSM_HEAD_DIM)
            y_off = _dot(cm_b, st_b[:, gl]) * ecs_e[:, gl]
            new_states.append(_dot(bm.T.astype(BF16), xdd_b[:, gl]))
            diag = []
            for pr in range(hpg // 2):
                xp = xd_b[:, gl][:, pr * LANES:(pr + 1) * LANES]
                res = []
                for hh in range(2):
                    hd = g * hpg + 2 * pr + hh
                    seg = cs[:, MISC_DT + hd:MISC_DT + hd + 1] - cs_t[MISC_DT + hd:MISC_DT + hd + 1, :]
                    decay = jnp.exp(jnp.where(tril, seg, -jnp.inf))
                    res.append(_dot((cb * decay).astype(BF16), xp))
                diag.append(jnp.where(left, res[0], res[1]))
            y_parts.append(jnp.concatenate(diag, axis=-1) + y_off)
        y = y + jnp.concatenate(y_parts, axis=-1)
        st_ref[...] = st_old * ecs_e[CHUNK - 1:CHUNK, :] + jnp.concatenate(new_states, axis=-1)

        yz = y * _silu(z_ref[0, r0:r0 + CHUNK, :].astype(F32))
        yn = yz * lax.rsqrt(jnp.mean(yz * yz, axis=-1, keepdims=True) + EPS) * gn_ref[...]
        o_ref[0, r0:r0 + CHUNK, :] = yn.astype(BF16)


def _cumsum_rows(tril_b, x):
    hi = x.astype(BF16)
    r1 = x - hi.astype(F32)
    mid = r1.astype(BF16)
    lo = (r1 - mid.astype(F32)).astype(BF16)
    return _dot(tril_b, hi) + _dot(tril_b, mid) + _dot(tril_b, lo)


def _ssd(xbc, z, misc, cw, cb, dtb, alog, dsk, gn, ts):
    bsz, s, _ = xbc.shape
    blk = lambda w: pl.BlockSpec((1, ts, w), lambda bi, j: (bi, j, 0))
    return pl.pallas_call(
        _ssd_kernel,
        grid=(bsz, s // ts),
        in_specs=[blk(SSM_XBC), blk(SSM_INNER), blk(LANES),
                  _const_spec((HALO, SSM_XBC)), _const_spec((1, SSM_XBC)),
                  _const_spec((1, LANES)), _const_spec((1, LANES)),
                  _const_spec((1, SSM_INNER)), _const_spec((1, SSM_INNER))],
        out_specs=blk(SSM_INNER),
        out_shape=jax.ShapeDtypeStruct((bsz, s, SSM_INNER), BF16),
        scratch_shapes=[pltpu.VMEM((ts + HALO, SSM_XBC), F32),
                        pltpu.VMEM((SSM_STATE, SSM_INNER), F32)],
        compiler_params=pltpu.CompilerParams(
            dimension_semantics=("arbitrary", "arbitrary"), vmem_limit_bytes=VMEM_LIMIT),
        name="ssd",
    )(xbc, z, misc, cw, cb, dtb, alog, dsk, gn)


FF_CHUNK = D_FF // 2


def _ffn_kernel(x_ref, u_ref, att_ref, y_ref, wo_ref, g_ref, wup_ref, cw_ref, cb_ref, wdn_ref,
                o_ref, ext_a, ext_g, carry_ref, gated_ref):
    tm = x_ref.shape[1]
    j = pl.program_id(1)

    @pl.when(j == 0)
    def _():
        carry_ref[...] = jnp.zeros(carry_ref.shape, F32)

    x1 = (x_ref[0]
          + _dot(u_ref[0], wo_ref[0:CONV_CH, :])
          + _dot(att_ref[0], wo_ref[CONV_CH:CONV_CH + ATT_W, :])
          + _dot(y_ref[0], wo_ref[CONV_CH + ATT_W:D_MIX, :]))
    hb = (x1 * lax.rsqrt(jnp.mean(x1 * x1, axis=-1, keepdims=True) + EPS) * g_ref[...]).astype(BF16)

    off = HALO - (FFN_CONV_K - 1)

    def conv_half(ext, c0):
        ext[0:HALO, :] = carry_ref[:, c0:c0 + FF_CHUNK]
        ext[HALO:, :] = _dot(hb, wup_ref[:, c0:c0 + FF_CHUNK])
        carry_ref[:, c0:c0 + FF_CHUNK] = ext[tm:tm + HALO, :]
        out = jnp.broadcast_to(cb_ref[:, c0:c0 + FF_CHUNK], (tm, FF_CHUNK))
        for tap in range(FFN_CONV_K):
            out = out + cw_ref[tap:tap + 1, c0:c0 + FF_CHUNK] * ext[off + tap:off + tap + tm, :]
        return out

    for c in range(D_FF // FF_CHUNK):
        fa = conv_half(ext_a, c * FF_CHUNK)
        fg = conv_half(ext_g, D_FF + c * FF_CHUNK)
        gated_ref[:, c * FF_CHUNK:(c + 1) * FF_CHUNK] = (_silu(fg) * fa).astype(BF16)
    o_ref[0] = x1 + _dot(gated_ref[...], wdn_ref[...])


def _ffn(x, u, att, y, wo, g, wup, cw, cb, wdn, tm):
    bsz, s, d = x.shape
    blk = lambda w: pl.BlockSpec((1, tm, w), lambda bi, j: (bi, j, 0))
    single = lambda shape: pl.BlockSpec(shape, lambda *_: (0,) * len(shape),
                                        pipeline_mode=pl.Buffered(1))
    return pl.pallas_call(
        _ffn_kernel,
        grid=(bsz, s // tm),
        in_specs=[blk(d), blk(CONV_CH), blk(ATT_W), blk(SSM_INNER),
                  single((D_MIX, d)), _const_spec((1, d)),
                  single((d, 2 * D_FF)), _const_spec((HALO, 2 * D_FF)), _const_spec((1, 2 * D_FF)),
                  single((D_FF, d))],
        out_specs=blk(d),
        out_shape=jax.ShapeDtypeStruct((bsz, s, d), F32),
        scratch_shapes=[pltpu.VMEM((tm + HALO, FF_CHUNK), F32),
                        pltpu.VMEM((tm + HALO, FF_CHUNK), F32),
                        pltpu.VMEM((HALO, 2 * D_FF), F32),
                        pltpu.VMEM((tm, D_FF), BF16)],
        compiler_params=pltpu.CompilerParams(
            dimension_semantics=("arbitrary", "arbitrary"), vmem_limit_bytes=VMEM_LIMIT),
        name="ffn",
    )(x, u, att, y, wo, g, wup, cw, cb, wdn)


def _t5_bucket(rel):
    n = jnp.maximum(rel, 0)
    max_exact = NUM_BUCKETS // 2
    nf = jnp.maximum(n, 1).astype(F32)
    large = max_exact + (jnp.log(nf / max_exact) / math.log(MAX_DISTANCE / max_exact)
                         * (NUM_BUCKETS - max_exact)).astype(jnp.int32)
    large = jnp.minimum(large, NUM_BUCKETS - 1)
    return jnp.where(n < max_exact, n, large)


def _bias_tables(rel_bias):
    tq = jnp.arange(QB)[:, None]
    kk = jnp.arange(QB)[None, :]
    rel = jnp.stack([tq + QB - kk, tq - kk])
    near = jnp.transpose(rel_bias[_t5_bucket(rel)].astype(F32), (0, 3, 1, 2))
    far = rel_bias[_t5_bucket(jnp.int32(QB + 1))].astype(F32)
    return near, far


def _pad_rows(w, rows):
    return jnp.concatenate([w, jnp.zeros((w.shape[0], rows - w.shape[1], w.shape[2]), w.dtype)], axis=1)


def _pad_lanes(v, start, width=LANES):
    out = jnp.zeros(v.shape[:-1] + (width,), v.dtype)
    return out.at[..., start:start + v.shape[-1]].set(v)


def kernel(x, rel_bias, norm_mix_g, w_in, conv_dw_w, conv_dw_b, conv_ln_g, conv_ln_b, q_norm_g, k_norm_g, ssm_conv_w, ssm_conv_b, dt_bias, a_log, d_skip, ssm_norm_g, w_out, norm_ffn_g, w_up, ffn_conv_w, ffn_conv_b, w_down):
    bsz, s, d = x.shape
    depth = w_in.shape[0]
    assert s % KC == 0 and s >= 2 * QB

    sizes = [CONV_CH, CONV_CH, ATT_W, ATT_W, ATT_W, N_IDX_HEADS * IDX_DIM, IDX_DIM, N_IDX_HEADS,
             SSM_INNER, SSM_XBC, SSM_HEADS]
    splits = [int(v) for v in np.cumsum(sizes)[:-1]]
    ca, cg, wq, wk, wv, wiq, wik, wiw, wz, wxbc, wdt = jnp.split(w_in, splits, axis=-1)
    w_main = jnp.concatenate([ca, cg, wq, wk, wv, wiq, wik, wik, wz, wxbc], axis=-1).astype(BF16)
    w_misc = jnp.concatenate(
        [_pad_lanes(wiw, 0, MISC_DT), _pad_lanes(wdt, 0, LANES - MISC_DT)], axis=-1).astype(BF16)
    seg = jnp.asarray(np.kron(np.eye(N_HEADS), np.full((HEAD_DIM, HEAD_DIM), 1.0 / HEAD_DIM)), BF16)

    near_bias, far_bias = _bias_tables(rel_bias)

    layer_params = dict(
        g_mix=norm_mix_g[:, None, :], w_main=w_main, w_misc=w_misc,
        gq=jnp.tile(q_norm_g, (1, N_HEADS))[:, None, :], gk=jnp.tile(k_norm_g, (1, N_HEADS))[:, None, :],
        conv_w=_pad_rows(conv_dw_w, CONV_HALO), conv_b=conv_dw_b[:, None, :],
        ln_g=conv_ln_g[:, None, :], ln_b=conv_ln_b[:, None, :],
        ssm_w=_pad_rows(ssm_conv_w, HALO), ssm_b=ssm_conv_b[:, None, :],
        dtb=_pad_lanes(dt_bias, MISC_DT)[:, None, :], alog=_pad_lanes(a_log, MISC_DT)[:, None, :],
        dsk=jnp.repeat(d_skip, SSM_HEAD_DIM, axis=-1)[:, None, :], gn=ssm_norm_g[:, None, :],
        w_out=w_out.astype(BF16), g_ffn=norm_ffn_g[:, None, :], w_up=w_up.astype(BF16),
        ffn_w=_pad_rows(ffn_conv_w, HALO), ffn_b=ffn_conv_b[:, None, :], w_down=w_down.astype(BF16),
    )

    tm = min(512, s)

    def layer(xc, p):
        (cacg, q, k, v, iq, ik2, z, xbc, misc) = _in_proj(
            xc.reshape(bsz * s, d), p["g_mix"], p["w_main"], p["w_misc"], p["gq"], p["gk"], seg, tm)
        r3 = lambda a: a.reshape(bsz, s, a.shape[-1])
        u = _conv_module(r3(cacg), p["conv_w"], p["conv_b"], p["ln_g"], p["ln_b"], tm)
        att = _dsa(far_bias, r3(q), r3(k), r3(v), r3(iq), r3(ik2), r3(misc), near_bias)
        y = _ssd(r3(xbc), r3(z), r3(misc), p["ssm_w"], p["ssm_b"], p["dtb"], p["alog"],
                 p["dsk"], p["gn"], tm)
        xn = _ffn(xc, u, att, y, p["w_out"], p["g_ffn"], p["w_up"], p["ffn_w"], p["ffn_b"],
                  p["w_down"], tm)
        return xn, None

    out, _ = lax.scan(layer, x, layer_params)
    return out
```

```python
import functools
import math

import jax
import jax.numpy as jnp
import numpy as np
from jax import lax
from jax.experimental import pallas as pl
from jax.experimental.pallas import tpu as pltpu

F32 = jnp.float32
BF16 = jnp.bfloat16

EPS = 1e-6
LANES = 128
CONV_CH = 512
CONV_K = 31
N_HEADS = 8
HEAD_DIM = 64
ATT_W = N_HEADS * HEAD_DIM
N_IDX_HEADS = 4
IDX_DIM = 64
TOPK_MAX = 256
NUM_BUCKETS = 32
MAX_DISTANCE = 128
SSM_HEADS = 8
SSM_HEAD_DIM = 64
SSM_INNER = SSM_HEADS * SSM_HEAD_DIM
SSM_GROUPS = 2
SSM_STATE = 128
SSM_CONV_K = 4
SSM_XBC = SSM_INNER + 2 * SSM_GROUPS * SSM_STATE
CHUNK = 128
D_MIX = CONV_CH + ATT_W + SSM_INNER
D_FF = 2816
FFN_CONV_K = 3

MISC_IW = 0
MISC_DT = 8
HALO = 8
CONV_HALO = 32
QB = 256
KC = 512
NEG = -1e30
VMEM_LIMIT = 56 * 1024 * 1024

NT_DIMS = (((1,), (1,)), ((), ()))


def _dot(a, b):
    return jnp.dot(a, b, preferred_element_type=F32)


def _dot_nt(a, b):
    return lax.dot_general(a, b, NT_DIMS, preferred_element_type=F32)


def _dot_split(x, m_bf16):
    hi = x.astype(BF16)
    r1 = x - hi.astype(F32)
    mid = r1.astype(BF16)
    lo = (r1 - mid.astype(F32)).astype(BF16)
    return _dot(hi, m_bf16) + _dot(mid, m_bf16) + _dot(lo, m_bf16)


def _silu(x):
    return x * jax.nn.sigmoid(x)


def _const_spec(shape):
    n = len(shape)
    return pl.BlockSpec(shape, lambda *_: (0,) * n)


def _in_proj_kernel(x_ref, g_ref, w_ref, wm_ref, wvt_ref, gq_ref, gk_ref, seg_ref,
                    cacg_ref, q_ref, k_ref, vt_ref, iq_ref, ik_ref, z_ref, xbc_ref, misc_ref):
    x = x_ref[...]
    h = x * lax.rsqrt(jnp.mean(x * x, axis=-1, keepdims=True) + EPS) * g_ref[...]
    hb = h.astype(BF16)

    def proj(c0, width):
        return _dot(hb, w_ref[:, c0:c0 + width])

    cacg_ref[...] = proj(0, 1024).astype(BF16)

    def qk_norm(c0, gain_ref, scale):
        t = proj(c0, ATT_W)
        ms = _dot_split(t * t, seg_ref[...])
        return (t * lax.rsqrt(ms + EPS) * (gain_ref[...] * scale)).astype(BF16)

    q_ref[...] = qk_norm(1024, gq_ref, HEAD_DIM ** -0.5)
    k_ref[...] = qk_norm(1536, gk_ref, 1.0)
    vt = _dot_nt(wvt_ref[...], hb).astype(BF16)
    for j in range(vt_ref.shape[0]):
        vt_ref[j] = vt[:, j * LANES:(j + 1) * LANES]
    iq_ref[...] = proj(2048, N_IDX_HEADS * IDX_DIM).astype(BF16)
    ik_ref[...] = proj(2304, LANES).astype(BF16)
    z_ref[...] = proj(2432, SSM_INNER).astype(BF16)
    xbc_ref[...] = proj(2944, SSM_XBC).astype(BF16)
    misc_ref[...] = _dot(hb, wm_ref[...])


W_MAIN_COLS = 2944 + SSM_XBC


def _in_proj(x2, g, w_main, w_misc, w_vt, gq, gk, seg, tm):
    t, d = x2.shape
    rows = lambda w, dt: (pl.BlockSpec((tm, w), lambda i: (i, 0)), jax.ShapeDtypeStruct((t, w), dt))
    vt_out = (pl.BlockSpec((tm // LANES, ATT_W, LANES), lambda i: (i, 0, 0)),
              jax.ShapeDtypeStruct((t // LANES, ATT_W, LANES), BF16))
    outs = [rows(1024, BF16), rows(ATT_W, BF16), rows(ATT_W, BF16), vt_out,
            rows(N_IDX_HEADS * IDX_DIM, BF16), rows(LANES, BF16), rows(SSM_INNER, BF16),
            rows(SSM_XBC, BF16), rows(LANES, F32)]
    return pl.pallas_call(
        _in_proj_kernel,
        grid=(t // tm,),
        in_specs=[pl.BlockSpec((tm, d), lambda i: (i, 0)),
                  _const_spec((1, d)),
                  _const_spec((d, W_MAIN_COLS)),
                  _const_spec((d, LANES)),
                  _const_spec((ATT_W, d)),
                  _const_spec((1, ATT_W)), _const_spec((1, ATT_W)),
                  _const_spec((ATT_W, ATT_W))],
        out_specs=[spec for spec, _ in outs],
        out_shape=[shape for _, shape in outs],
        compiler_params=pltpu.CompilerParams(
            dimension_semantics=("arbitrary",), vmem_limit_bytes=VMEM_LIMIT),
        name="in_proj",
    )(x2, g, w_main, w_misc, w_vt, gq, gk, seg)


CONV_ROWS = 64


def _conv_kernel(cacg_ref, w_ref, b_ref, g_ref, beta_ref, o_ref, ext_ref):
    ts = o_ref.shape[1]
    j = pl.program_id(1)

    @pl.when(j == 0)
    def _():
        ext_ref[0:CONV_HALO, :] = jnp.zeros((CONV_HALO, CONV_CH), F32)

    @pl.when(j > 0)
    def _():
        ext_ref[0:CONV_HALO, :] = ext_ref[ts:ts + CONV_HALO, :]

    ca = cacg_ref[0, :, 0:CONV_CH].astype(F32)
    cg = cacg_ref[0, :, CONV_CH:2 * CONV_CH].astype(F32)
    ext_ref[CONV_HALO:, :] = ca * jax.nn.sigmoid(cg)

    w = w_ref[...]
    off = CONV_HALO - (CONV_K - 1)
    for r in range(ts // CONV_ROWS):
        base = r * CONV_ROWS
        acc = jnp.broadcast_to(b_ref[...], (CONV_ROWS, CONV_CH))
        for tap in range(CONV_K):
            acc = acc + w[tap:tap + 1, :] * ext_ref[base + off + tap:base + off + tap + CONV_ROWS, :]
        mu = jnp.mean(acc, axis=-1, keepdims=True)
        cen = acc - mu
        var = jnp.mean(cen * cen, axis=-1, keepdims=True)
        yn = cen * lax.rsqrt(var + EPS) * g_ref[...] + beta_ref[...]
        o_ref[0, base:base + CONV_ROWS, :] = _silu(yn).astype(BF16)


def _conv_module(cacg, w, b, g, beta, ts):
    bsz, s, _ = cacg.shape
    return pl.pallas_call(
        _conv_kernel,
        grid=(bsz, s // ts),
        in_specs=[pl.BlockSpec((1, ts, 2 * CONV_CH), lambda bi, j: (bi, j, 0)),
                  _const_spec((CONV_HALO, CONV_CH)),
                  _const_spec((1, CONV_CH)), _const_spec((1, CONV_CH)), _const_spec((1, CONV_CH))],
        out_specs=pl.BlockSpec((1, ts, CONV_CH), lambda bi, j: (bi, j, 0)),
        out_shape=jax.ShapeDtypeStruct((bsz, s, CONV_CH), BF16),
        scratch_shapes=[pltpu.VMEM((ts + CONV_HALO, CONV_CH), F32)],
        compiler_params=pltpu.CompilerParams(
            dimension_semantics=("arbitrary", "arbitrary"), vmem_limit_bytes=VMEM_LIMIT),
        name="conv_module",
    )(cacg, w, b, g, beta)


def _key_to_float(key):
    bits = key ^ ((key >> 31) & jnp.int32(0x7FFFFFFF))
    return lax.bitcast_convert_type(bits, F32)


KEY_NEG_INF = int(np.int32(np.uint32(0xFF800000)) ^ np.int32(0x7FFFFFFF))
KEY_POS_INF_NEXT = 0x7F800000 + 1


def _dsa_kernel(far_ref, q_ref, k_ref, vt_ref, iq_ref, ik_ref, misc_ref, nb_ref, o_ref,
                sc_ref, qm_ref, acc_ref, thr_ref, pos_ref, *, topk):
    i = pl.program_id(1)
    s_len = k_ref.shape[1]
    q0 = i * QB
    lane = lax.broadcasted_iota(jnp.int32, (LANES, QB), 1)
    row = lax.broadcasted_iota(jnp.int32, (LANES, QB), 0)
    left = lax.broadcasted_iota(jnp.int32, (QB, LANES), 1) < HEAD_DIM
    qpos = q0 + lane

    iq = iq_ref[0]
    misc_t = misc_ref[0].T
    iqm, iw = [], []
    for h in range(N_IDX_HEADS):
        pair = iq[:, (h // 2) * LANES:(h // 2 + 1) * LANES]
        iqm.append(jnp.where(left if h % 2 == 0 else ~left, pair, jnp.zeros_like(pair)))
        iw.append(misc_t[MISC_IW + h:MISC_IW + h + 1, :] * (IDX_DIM ** -0.5 * N_IDX_HEADS ** -0.5))

    def score_chunk(c, carry):
        k0 = pl.multiple_of(c * KC, KC)
        ikc = ik_ref[0, pl.ds(k0, KC), :]
        s = jnp.zeros((KC, QB), F32)
        for h in range(N_IDX_HEADS):
            s = s + jnp.maximum(_dot_nt(ikc, iqm[h]), 0.0) * iw[h]
        for t in range(KC // LANES):
            kpos = k0 + t * LANES + row
            sc_ref[c * (KC // LANES) + t] = jnp.where(
                kpos <= qpos, s[t * LANES:(t + 1) * LANES, :], -jnp.inf)
        return carry

    per_chunk = KC // LANES
    n_chunks = (q0 + QB - 1) // KC + 1
    lax.fori_loop(0, n_chunks, score_chunk, 0)

    def count_blocks(pred):
        def body(c, acc):
            for t in range(per_chunk):
                bk = c * per_chunk + t
                acc = acc + jnp.where(pred(sc_ref[bk], bk), 1.0, 0.0)
            return acc
        acc = lax.fori_loop(0, n_chunks, body, jnp.zeros((LANES, QB), F32))
        return jnp.sum(acc, axis=0, keepdims=True)

    def bisect(_, lohi):
        lo, hi = lohi
        mid = (lo >> 1) + (hi >> 1) + (lo & hi & 1)
        thr_b = jnp.broadcast_to(_key_to_float(mid), (LANES, QB))
        ok = count_blocks(lambda s, bk: s >= thr_b) >= topk
        return jnp.where(ok, mid, lo), jnp.where(ok, hi, mid)

    lo0 = jnp.full((1, QB), KEY_NEG_INF, jnp.int32)
    hi0 = jnp.full((1, QB), KEY_POS_INF_NEXT, jnp.int32)
    lo, _ = lax.fori_loop(0, 32, bisect, (lo0, hi0))
    thr = _key_to_float(lo)
    thr_b = jnp.broadcast_to(thr, (LANES, QB))
    n_gt = count_blocks(lambda s, bk: s > thr_b)
    n_ge = count_blocks(lambda s, bk: s >= thr_b)
    thr_ref[...] = thr_b
    pos_ref[...] = jnp.full((LANES, QB), s_len, jnp.int32)

    @pl.when(jnp.max(n_ge) > topk)
    def _():
        need = topk - n_gt

        def bisect_pos(_, lohi):
            plo, phi = lohi
            mid = (plo + phi) >> 1
            mid_b = jnp.broadcast_to(mid, (LANES, QB))
            cnt = count_blocks(lambda s, bk: (s == thr_b) & (bk * LANES + row <= mid_b))
            ok = cnt >= need
            return jnp.where(ok, plo, mid), jnp.where(ok, mid, phi)

        plo0 = jnp.full((1, QB), -1, jnp.int32)
        phi0 = jnp.full((1, QB), n_chunks * KC - 1, jnp.int32)
        n_iter = int(math.ceil(math.log2(s_len))) + 1
        _, phi = lax.fori_loop(0, n_iter, bisect_pos, (plo0, phi0))
        pos_ref[...] = jnp.broadcast_to(phi, (LANES, QB))

    q = q_ref[0]
    for h in range(N_HEADS):
        pair = q[:, (h // 2) * LANES:(h // 2 + 1) * LANES]
        qm_ref[h] = jnp.where(left if h % 2 == 0 else ~left, pair, jnp.zeros_like(pair))
    acc_ref[...] = jnp.zeros(acc_ref.shape, F32)

    def mask_tile(bk, limit):
        s = sc_ref[bk]
        t = thr_ref[...]
        kpos = bk * LANES + row
        sel = (s > t) | ((s == t) & (kpos <= pos_ref[...]))
        sel = sel & (kpos < limit) & (kpos <= qpos)
        return jnp.where(sel, 0.0, NEG)

    def attend(blocks, maskadd, near, stats):
        width = len(blocks) * LANES
        k0 = pl.multiple_of(blocks[0] * LANES, LANES)
        new_stats = []
        for h in range(N_HEADS):
            kp = k_ref[0, pl.ds(k0, width), (h // 2) * LANES:(h // 2 + 1) * LANES]
            lg = _dot_nt(kp, qm_ref[h]) + maskadd
            if near is None:
                shift = far_ref[h]
            else:
                lg = lg + nb_ref[h, near:near + width, :]
                shift = 0.0
            m_old, l_old = stats[h]
            m_new = jnp.maximum(m_old, jnp.max(lg, axis=0, keepdims=True) + shift)
            alpha = jnp.exp(m_old - m_new)
            p = jnp.exp(lg - (m_new - shift))
            new_stats.append((m_new, alpha * l_old + jnp.sum(p, axis=0, keepdims=True)))
            hs = slice(h * HEAD_DIM, (h + 1) * HEAD_DIM)
            vt = jnp.concatenate([vt_ref[0, bk, hs, :] for bk in blocks], axis=1)
            acc_ref[hs, :] = alpha * acc_ref[hs, :] + _dot(vt, p.astype(BF16))
        return new_stats

    stats = [(jnp.full((1, QB), NEG, F32), jnp.zeros((1, QB), F32)) for _ in range(N_HEADS)]

    far_end = jnp.maximum(q0 - LANES, 0)

    def far_chunk(c, stats):
        blocks = [c * per_chunk + t for t in range(per_chunk)]
        maskadd = jnp.concatenate([mask_tile(bk, far_end) for bk in blocks], axis=0)
        return attend(blocks, maskadd, None, stats)

    stats = lax.fori_loop(0, (far_end + KC - 1) // KC, far_chunk, stats)

    diag = [q0 // LANES + t for t in range(QB // LANES)]
    prev = jnp.maximum(q0 // LANES - 1, 0)
    stats = attend([prev], mask_tile(prev, jnp.where(i >= 1, s_len, 0)), 0, stats)
    stats = attend(diag, jnp.concatenate([mask_tile(bk, s_len) for bk in diag], axis=0), LANES, stats)

    out_t = jnp.concatenate(
        [acc_ref[h * HEAD_DIM:(h + 1) * HEAD_DIM, :] / stats[h][1] for h in range(N_HEADS)],
        axis=0)
    o_ref[0] = out_t.T.astype(BF16)


def _dsa(far_bias, q, k, vt, iq, ik2, misc, near_bias):
    bsz, s, _ = q.shape
    topk = min(TOPK_MAX, s // 4)
    nq = s // QB
    blk = lambda w: pl.BlockSpec((1, QB, w), lambda bi, i: (bi, i, 0))
    once = pl.Buffered(1)
    full = lambda w: pl.BlockSpec((1, s, w), lambda bi, i: (bi, 0, 0), pipeline_mode=once)
    return pl.pallas_call(
        functools.partial(_dsa_kernel, topk=topk),
        grid=(bsz, nq),
        in_specs=[pl.BlockSpec(memory_space=pltpu.SMEM),
                  blk(ATT_W), full(ATT_W),
                  pl.BlockSpec((1, s // LANES, ATT_W, LANES), lambda bi, i: (bi, 0, 0, 0),
                               pipeline_mode=once),
                  blk(N_IDX_HEADS * IDX_DIM), full(LANES), blk(LANES),
                  pl.BlockSpec((N_HEADS, LANES + QB, QB), lambda bi, i: (0, 0, 0), pipeline_mode=once)],
        out_specs=blk(ATT_W),
        out_shape=jax.ShapeDtypeStruct((bsz, s, ATT_W), BF16),
        scratch_shapes=[
            pltpu.VMEM((_round_up(s, KC) // LANES, LANES, QB), F32),
            pltpu.VMEM((N_HEADS, QB, LANES), BF16),
            pltpu.VMEM((ATT_W, QB), F32),
            pltpu.VMEM((LANES, QB), F32),
            pltpu.VMEM((LANES, QB), jnp.int32),
        ],
        compiler_params=pltpu.CompilerParams(
            dimension_semantics=("arbitrary", "arbitrary"), vmem_limit_bytes=VMEM_LIMIT),
        name="dsa",
    )(far_bias, q, k, vt, iq, ik2, misc, near_bias)


def _round_up(a, b):
    return (a + b - 1) // b * b


def _ssd_kernel(xbc_ref, z_ref, misc_ref, cw_ref, cb_ref, dtb_ref, alog_ref, dsk_ref, gn_ref,
                o_ref, ext_ref, st_ref):
    ts = o_ref.shape[1]
    j = pl.program_id(1)

    @pl.when(j == 0)
    def _():
        ext_ref[0:HALO, :] = jnp.zeros((HALO, SSM_XBC), F32)
        st_ref[...] = jnp.zeros(st_ref.shape, F32)

    @pl.when(j > 0)
    def _():
        ext_ref[0:HALO, :] = ext_ref[ts:ts + HALO, :]

    ext_ref[HALO:, :] = xbc_ref[0].astype(F32)

    lane = lax.broadcasted_iota(jnp.int32, (CHUNK, LANES), 1)
    row = lax.broadcasted_iota(jnp.int32, (CHUNK, LANES), 0)
    tril = row >= lane
    tril_b = jnp.where(tril, 1.0, 0.0).astype(BF16)
    is_dt = (lane >= MISC_DT) & (lane < MISC_DT + SSM_HEADS)
    e_row = lax.broadcasted_iota(jnp.int32, (LANES, SSM_INNER), 0)
    e_col = lax.broadcasted_iota(jnp.int32, (LANES, SSM_INNER), 1)
    expand = jnp.where(e_row - MISC_DT == e_col // SSM_HEAD_DIM, 1.0, 0.0).astype(BF16)
    left = lane < SSM_HEAD_DIM
    lane1 = lax.broadcasted_iota(jnp.int32, (1, LANES), 1)
    a_row = jnp.where((lane1 >= MISC_DT) & (lane1 < MISC_DT + SSM_HEADS), -jnp.exp(alog_ref[...]), 0.0)
    cw = cw_ref[...]
    off = HALO - (SSM_CONV_K - 1)
    gw = SSM_GROUPS * SSM_STATE

    for c in range(ts // CHUNK):
        r0 = c * CHUNK
        conv = jnp.broadcast_to(cb_ref[...], (CHUNK, SSM_XBC))
        for tap in range(SSM_CONV_K):
            conv = conv + cw[tap:tap + 1, :] * ext_ref[r0 + off + tap:r0 + off + tap + CHUNK, :]
        xc = _silu(conv)
        xs = xc[:, 0:SSM_INNER]

        dt = jnp.where(is_dt, jax.nn.softplus(misc_ref[0, r0:r0 + CHUNK, :] + dtb_ref[...]), 0.0)
        cs = _cumsum_rows(tril_b, dt * a_row)
        cs_last = cs[CHUNK - 1:CHUNK, :]
        dt_e = _dot_split(dt, expand)
        dec_e = _dot_split(jnp.exp(cs_last - cs), expand)
        ecs_e = _dot_split(jnp.exp(cs), expand)
        cs_t = cs.T

        xd = xs * dt_e
        xd_b = xd.astype(BF16)
        xdd_b = (xd * dec_e).astype(BF16)
        y = dsk_ref[...] * xs
        st_old = st_ref[...]
        st_b = st_old.astype(BF16)
        new_states = []
        y_parts = []
        for g in range(SSM_GROUPS):
            bm = xc[:, SSM_INNER + g * SSM_STATE:SSM_INNER + (g + 1) * SSM_STATE]
            cm_b = xc[:, SSM_INNER + gw + g * SSM_STATE:SSM_INNER + gw + (g + 1) * SSM_STATE].astype(BF16)
            cb = _dot_nt(cm_b, bm.astype(BF16))
            hpg = SSM_HEADS // SSM_GROUPS
            gl = slice(g * hpg * SSM_HEAD_DIM, (g + 1) * hpg * SSM_HEAD_DIM)
            y_off = _dot(cm_b, st_b[:, gl]) * ecs_e[:, gl]
            new_states.append(_dot(bm.T.astype(BF16), xdd_b[:, gl]))
            diag = []
            for pr in range(hpg // 2):
                xp = xd_b[:, gl][:, pr * LANES:(pr + 1) * LANES]
                res = []
                for hh in range(2):
                    hd = g * hpg + 2 * pr + hh
                    seg = cs[:, MISC_DT + hd:MISC_DT + hd + 1] - cs_t[MISC_DT + hd:MISC_DT + hd + 1, :]
                    decay = jnp.exp(jnp.where(tril, seg, -jnp.inf))
                    res.append(_dot((cb * decay).astype(BF16), xp))
                diag.append(jnp.where(left, res[0], res[1]))
            y_parts.append(jnp.concatenate(diag, axis=-1) + y_off)
        y = y + jnp.concatenate(y_parts, axis=-1)
        st_ref[...] = st_old * ecs_e[CHUNK - 1:CHUNK, :] + jnp.concatenate(new_states, axis=-1)

        yz = y * _silu(z_ref[0, r0:r0 + CHUNK, :].astype(F32))
        yn = yz * lax.rsqrt(jnp.mean(yz * yz, axis=-1, keepdims=True) + EPS) * gn_ref[...]
        o_ref[0, r0:r0 + CHUNK, :] = yn.astype(BF16)


def _cumsum_rows(tril_b, x):
    hi = x.astype(BF16)
    r1 = x - hi.astype(F32)
    mid = r1.astype(BF16)
    lo = (r1 - mid.astype(F32)).astype(BF16)
    return _dot(tril_b, hi) + _dot(tril_b, mid) + _dot(tril_b, lo)


def _ssd(xbc, z, misc, cw, cb, dtb, alog, dsk, gn, ts):
    bsz, s, _ = xbc.shape
    blk = lambda w: pl.BlockSpec((1, ts, w), lambda bi, j: (bi, j, 0))
    return pl.pallas_call(
        _ssd_kernel,
        grid=(bsz, s // ts),
        in_specs=[blk(SSM_XBC), blk(SSM_INNER), blk(LANES),
                  _const_spec((HALO, SSM_XBC)), _const_spec((1, SSM_XBC)),
                  _const_spec((1, LANES)), _const_spec((1, LANES)),
                  _const_spec((1, SSM_INNER)), _const_spec((1, SSM_INNER))],
        out_specs=blk(SSM_INNER),
        out_shape=jax.ShapeDtypeStruct((bsz, s, SSM_INNER), BF16),
        scratch_shapes=[pltpu.VMEM((ts + HALO, SSM_XBC), F32),
                        pltpu.VMEM((SSM_STATE, SSM_INNER), F32)],
        compiler_params=pltpu.CompilerParams(
            dimension_semantics=("arbitrary", "arbitrary"), vmem_limit_bytes=VMEM_LIMIT),
        name="ssd",
    )(xbc, z, misc, cw, cb, dtb, alog, dsk, gn)


FF_CHUNK = D_FF // 2


def _ffn_kernel(x_ref, u_ref, att_ref, y_ref, wo_ref, g_ref, wup_ref, cw_ref, cb_ref, wdn_ref,
                o_ref, ext_a, ext_g, carry_ref, gated_ref):
    tm = x_ref.shape[1]
    j = pl.program_id(1)

    @pl.when(j == 0)
    def _():
        carry_ref[...] = jnp.zeros(carry_ref.shape, F32)

    x1 = (x_ref[0]
          + _dot(u_ref[0], wo_ref[0:CONV_CH, :])
          + _dot(att_ref[0], wo_ref[CONV_CH:CONV_CH + ATT_W, :])
          + _dot(y_ref[0], wo_ref[CONV_CH + ATT_W:D_MIX, :]))
    hb = (x1 * lax.rsqrt(jnp.mean(x1 * x1, axis=-1, keepdims=True) + EPS) * g_ref[...]).astype(BF16)

    off = HALO - (FFN_CONV_K - 1)

    def conv_half(ext, c0):
        ext[0:HALO, :] = carry_ref[:, c0:c0 + FF_CHUNK]
        ext[HALO:, :] = _dot(hb, wup_ref[:, c0:c0 + FF_CHUNK])
        carry_ref[:, c0:c0 + FF_CHUNK] = ext[tm:tm + HALO, :]
        out = jnp.broadcast_to(cb_ref[:, c0:c0 + FF_CHUNK], (tm, FF_CHUNK))
        for tap in range(FFN_CONV_K):
            out = out + cw_ref[tap:tap + 1, c0:c0 + FF_CHUNK] * ext[off + tap:off + tap + tm, :]
        return out

    for c in range(D_FF // FF_CHUNK):
        fa = conv_half(ext_a, c * FF_CHUNK)
        fg = conv_half(ext_g, D_FF + c * FF_CHUNK)
        gated_ref[:, c * FF_CHUNK:(c + 1) * FF_CHUNK] = (_silu(fg) * fa).astype(BF16)
    o_ref[0] = x1 + _dot(gated_ref[...], wdn_ref[...])


def _ffn(x, u, att, y, wo, g, wup, cw, cb, wdn, tm):
    bsz, s, d = x.shape
    blk = lambda w: pl.BlockSpec((1, tm, w), lambda bi, j: (bi, j, 0))
    single = lambda shape: pl.BlockSpec(shape, lambda *_: (0,) * len(shape),
                                        pipeline_mode=pl.Buffered(1))
    return pl.pallas_call(
        _ffn_kernel,
        grid=(bsz, s // tm),
        in_specs=[blk(d), blk(CONV_CH), blk(ATT_W), blk(SSM_INNER),
                  single((D_MIX, d)), _const_spec((1, d)),
                  single((d, 2 * D_FF)), _const_spec((HALO, 2 * D_FF)), _const_spec((1, 2 * D_FF)),
                  single((D_FF, d))],
        out_specs=blk(d),
        out_shape=jax.ShapeDtypeStruct((bsz, s, d), F32),
        scratch_shapes=[pltpu.VMEM((tm + HALO, FF_CHUNK), F32),
                        pltpu.VMEM((tm + HALO, FF_CHUNK), F32),
                        pltpu.VMEM((HALO, 2 * D_FF), F32),
                        pltpu.VMEM((tm, D_FF), BF16)],
        compiler_params=pltpu.CompilerParams(
            dimension_semantics=("arbitrary", "arbitrary"), vmem_limit_bytes=VMEM_LIMIT),
        name="ffn",
    )(x, u, att, y, wo, g, wup, cw, cb, wdn)


def _t5_bucket(rel):
    n = jnp.maximum(rel, 0)
    max_exact = NUM_BUCKETS // 2
    nf = jnp.maximum(n, 1).astype(F32)
    large = max_exact + (jnp.log(nf / max_exact) / math.log(MAX_DISTANCE / max_exact)
                         * (NUM_BUCKETS - max_exact)).astype(jnp.int32)
    large = jnp.minimum(large, NUM_BUCKETS - 1)
    return jnp.where(n < max_exact, n, large)


def _bias_tables(rel_bias):
    kk = jnp.arange(LANES + QB)[:, None]
    tq = jnp.arange(QB)[None, :]
    rel = tq + LANES - kk
    near = jnp.transpose(rel_bias[_t5_bucket(rel)].astype(F32), (2, 0, 1))
    far = rel_bias[_t5_bucket(jnp.int32(LANES + 1))].astype(F32)
    return near, far


def _pad_rows(w, rows):
    return jnp.concatenate([w, jnp.zeros((w.shape[0], rows - w.shape[1], w.shape[2]), w.dtype)], axis=1)


def _pad_lanes(v, start, width=LANES):
    out = jnp.zeros(v.shape[:-1] + (width,), v.dtype)
    return out.at[..., start:start + v.shape[-1]].set(v)


def kernel(x, rel_bias, norm_mix_g, w_in, conv_dw_w, conv_dw_b, conv_ln_g, conv_ln_b, q_norm_g, k_norm_g, ssm_conv_w, ssm_conv_b, dt_bias, a_log, d_skip, ssm_norm_g, w_out, norm_ffn_g, w_up, ffn_conv_w, ffn_conv_b, w_down):
    bsz, s, d = x.shape
    depth = w_in.shape[0]
    assert s % KC == 0 and s >= 2 * QB

    sizes = [CONV_CH, CONV_CH, ATT_W, ATT_W, ATT_W, N_IDX_HEADS * IDX_DIM, IDX_DIM, N_IDX_HEADS,
             SSM_INNER, SSM_XBC, SSM_HEADS]
    splits = [int(v) for v in np.cumsum(sizes)[:-1]]
    ca, cg, wq, wk, wv, wiq, wik, wiw, wz, wxbc, wdt = jnp.split(w_in, splits, axis=-1)
    w_main = jnp.concatenate([ca, cg, wq, wk, wiq, wik, wik, wz, wxbc], axis=-1).astype(BF16)
    w_vt = jnp.swapaxes(wv, 1, 2).astype(BF16)
    w_misc = jnp.concatenate(
        [_pad_lanes(wiw, 0, MISC_DT), _pad_lanes(wdt, 0, LANES - MISC_DT)], axis=-1).astype(BF16)
    seg = jnp.asarray(np.kron(np.eye(N_HEADS), np.full((HEAD_DIM, HEAD_DIM), 1.0 / HEAD_DIM)), BF16)

    near_bias, far_bias = _bias_tables(rel_bias)

    layer_params = dict(
        g_mix=norm_mix_g[:, None, :], w_main=w_main, w_misc=w_misc, w_vt=w_vt,
        gq=jnp.tile(q_norm_g, (1, N_HEADS))[:, None, :], gk=jnp.tile(k_norm_g, (1, N_HEADS))[:, None, :],
        conv_w=_pad_rows(conv_dw_w, CONV_HALO), conv_b=conv_dw_b[:, None, :],
        ln_g=conv_ln_g[:, None, :], ln_b=conv_ln_b[:, None, :],
        ssm_w=_pad_rows(ssm_conv_w, HALO), ssm_b=ssm_conv_b[:, None, :],
        dtb=_pad_lanes(dt_bias, MISC_DT)[:, None, :], alog=_pad_lanes(a_log, MISC_DT)[:, None, :],
        dsk=jnp.repeat(d_skip, SSM_HEAD_DIM, axis=-1)[:, None, :], gn=ssm_norm_g[:, None, :],
        w_out=w_out.astype(BF16), g_ffn=norm_ffn_g[:, None, :], w_up=w_up.astype(BF16),
        ffn_w=_pad_rows(ffn_conv_w, HALO), ffn_b=ffn_conv_b[:, None, :], w_down=w_down.astype(BF16),
    )

    tm = min(512, s)

    def layer(xc, p):
        (cacg, q, k, vt, iq, ik2, z, xbc, misc) = _in_proj(
            xc.reshape(bsz * s, d), p["g_mix"], p["w_main"], p["w_misc"], p["w_vt"], p["gq"], p["gk"],
            seg, tm)
        r3 = lambda a: a.reshape(bsz, s, a.shape[-1])
        u = _conv_module(r3(cacg), p["conv_w"], p["conv_b"], p["ln_g"], p["ln_b"], tm)
        att = _dsa(far_bias, r3(q), r3(k), vt.reshape(bsz, s // LANES, ATT_W, LANES), r3(iq), r3(ik2),
                   r3(misc), near_bias)
        y = _ssd(r3(xbc), r3(z), r3(misc), p["ssm_w"], p["ssm_b"], p["dtb"], p["alog"],
                 p["dsk"], p["gn"], tm)
        xn = _ffn(xc, u, att, y, p["w_out"], p["g_ffn"], p["w_up"], p["ffn_w"], p["ffn_b"],
                  p["w_down"], tm)
        return xn, None

    out, _ = lax.scan(layer, x, layer_params)
    return out
```

```python
import functools
import math

import jax
import jax.numpy as jnp
import numpy as np
from jax import lax
from jax.experimental import pallas as pl
from jax.experimental.pallas import tpu as pltpu

F32 = jnp.float32
BF16 = jnp.bfloat16

EPS = 1e-6
LANES = 128
SUBLANES = 8
CONV_CH = 512
CONV_K = 31
N_HEADS = 8
HEAD_DIM = 64
ATT_W = N_HEADS * HEAD_DIM
N_IDX_HEADS = 4
IDX_DIM = 64
TOPK_MAX = 256
NUM_BUCKETS = 32
MAX_DISTANCE = 128
SSM_HEADS = 8
SSM_HEAD_DIM = 64
SSM_INNER = SSM_HEADS * SSM_HEAD_DIM
SSM_GROUPS = 2
SSM_STATE = 128
SSM_CONV_K = 4
SSM_XBC = SSM_INNER + 2 * SSM_GROUPS * SSM_STATE
CHUNK = 128
D_MIX = CONV_CH + ATT_W + SSM_INNER
D_FF = 2816
FFN_CONV_K = 3

MISC_IW = 0
MISC_DT = 8
HALO = 8
CONV_HALO = 32
QB = 256
KC = 512
NEG = -1e30
VMEM_LIMIT = 56 * 1024 * 1024

NT_DIMS = (((1,), (1,)), ((), ()))


def _dot(a, b):
    return jnp.dot(a, b, preferred_element_type=F32)


def _dot_nt(a, b):
    return lax.dot_general(a, b, NT_DIMS, preferred_element_type=F32)


def _dot_split(x, m_bf16):
    hi = x.astype(BF16)
    r1 = x - hi.astype(F32)
    mid = r1.astype(BF16)
    lo = (r1 - mid.astype(F32)).astype(BF16)
    return _dot(hi, m_bf16) + _dot(mid, m_bf16) + _dot(lo, m_bf16)


def _silu(x):
    return x * jax.nn.sigmoid(x)


def _const_spec(shape):
    n = len(shape)
    return pl.BlockSpec(shape, lambda *_: (0,) * n)


def _in_proj_kernel(x_ref, g_ref, w_ref, wm_ref, wvt_ref, gq_ref, gk_ref, seg_ref,
                    cacg_ref, q_ref, k_ref, vt_ref, iq_ref, ik_ref, z_ref, xbc_ref, misc_ref):
    x = x_ref[...]
    h = x * lax.rsqrt(jnp.mean(x * x, axis=-1, keepdims=True) + EPS) * g_ref[...]
    hb = h.astype(BF16)

    def proj(c0, width):
        return _dot(hb, w_ref[:, c0:c0 + width])

    cacg_ref[...] = proj(0, 1024).astype(BF16)

    def qk_norm(c0, gain_ref, scale):
        t = proj(c0, ATT_W)
        ms = _dot_split(t * t, seg_ref[...])
        return (t * lax.rsqrt(ms + EPS) * (gain_ref[...] * scale)).astype(BF16)

    q_ref[...] = qk_norm(1024, gq_ref, HEAD_DIM ** -0.5)
    k_ref[...] = qk_norm(1536, gk_ref, 1.0)
    vt = _dot_nt(wvt_ref[...], hb).astype(BF16)
    for j in range(vt_ref.shape[0]):
        vt_ref[j] = vt[:, j * LANES:(j + 1) * LANES]
    iq_ref[...] = proj(2048, N_IDX_HEADS * IDX_DIM).astype(BF16)
    ik_ref[...] = proj(2304, LANES).astype(BF16)
    z_ref[...] = proj(2432, SSM_INNER).astype(BF16)
    xbc_ref[...] = proj(2944, SSM_XBC).astype(BF16)
    misc_ref[...] = _dot(hb, wm_ref[...])


W_MAIN_COLS = 2944 + SSM_XBC


def _in_proj(x2, g, w_main, w_misc, w_vt, gq, gk, seg, tm):
    t, d = x2.shape
    rows = lambda w, dt: (pl.BlockSpec((tm, w), lambda i: (i, 0)), jax.ShapeDtypeStruct((t, w), dt))
    vt_out = (pl.BlockSpec((tm // LANES, ATT_W, LANES), lambda i: (i, 0, 0)),
              jax.ShapeDtypeStruct((t // LANES, ATT_W, LANES), BF16))
    outs = [rows(1024, BF16), rows(ATT_W, BF16), rows(ATT_W, BF16), vt_out,
            rows(N_IDX_HEADS * IDX_DIM, BF16), rows(LANES, BF16), rows(SSM_INNER, BF16),
            rows(SSM_XBC, BF16), rows(LANES, F32)]
    return pl.pallas_call(
        _in_proj_kernel,
        grid=(t // tm,),
        in_specs=[pl.BlockSpec((tm, d), lambda i: (i, 0)),
                  _const_spec((1, d)),
                  _const_spec((d, W_MAIN_COLS)),
                  _const_spec((d, LANES)),
                  _const_spec((ATT_W, d)),
                  _const_spec((1, ATT_W)), _const_spec((1, ATT_W)),
                  _const_spec((ATT_W, ATT_W))],
        out_specs=[spec for spec, _ in outs],
        out_shape=[shape for _, shape in outs],
        compiler_params=pltpu.CompilerParams(
            dimension_semantics=("arbitrary",), vmem_limit_bytes=VMEM_LIMIT),
        name="in_proj",
    )(x2, g, w_main, w_misc, w_vt, gq, gk, seg)


CONV_ROWS = 64


def _conv_kernel(cacg_ref, w_ref, b_ref, g_ref, beta_ref, o_ref, ext_ref):
    ts = o_ref.shape[1]
    j = pl.program_id(1)

    @pl.when(j == 0)
    def _():
        ext_ref[0:CONV_HALO, :] = jnp.zeros((CONV_HALO, CONV_CH), F32)

    @pl.when(j > 0)
    def _():
        ext_ref[0:CONV_HALO, :] = ext_ref[ts:ts + CONV_HALO, :]

    ca = cacg_ref[0, :, 0:CONV_CH].astype(F32)
    cg = cacg_ref[0, :, CONV_CH:2 * CONV_CH].astype(F32)
    ext_ref[CONV_HALO:, :] = ca * jax.nn.sigmoid(cg)

    w = w_ref[...]
    off = CONV_HALO - (CONV_K - 1)
    for r in range(ts // CONV_ROWS):
        base = r * CONV_ROWS
        acc = jnp.broadcast_to(b_ref[...], (CONV_ROWS, CONV_CH))
        for tap in range(CONV_K):
            acc = acc + w[tap:tap + 1, :] * ext_ref[base + off + tap:base + off + tap + CONV_ROWS, :]
        mu = jnp.mean(acc, axis=-1, keepdims=True)
        cen = acc - mu
        var = jnp.mean(cen * cen, axis=-1, keepdims=True)
        yn = cen * lax.rsqrt(var + EPS) * g_ref[...] + beta_ref[...]
        o_ref[0, base:base + CONV_ROWS, :] = _silu(yn).astype(BF16)


def _conv_module(cacg, w, b, g, beta, ts):
    bsz, s, _ = cacg.shape
    return pl.pallas_call(
        _conv_kernel,
        grid=(bsz, s // ts),
        in_specs=[pl.BlockSpec((1, ts, 2 * CONV_CH), lambda bi, j: (bi, j, 0)),
                  _const_spec((CONV_HALO, CONV_CH)),
                  _const_spec((1, CONV_CH)), _const_spec((1, CONV_CH)), _const_spec((1, CONV_CH))],
        out_specs=pl.BlockSpec((1, ts, CONV_CH), lambda bi, j: (bi, j, 0)),
        out_shape=jax.ShapeDtypeStruct((bsz, s, CONV_CH), BF16),
        scratch_shapes=[pltpu.VMEM((ts + CONV_HALO, CONV_CH), F32)],
        compiler_params=pltpu.CompilerParams(
            dimension_semantics=("arbitrary", "arbitrary"), vmem_limit_bytes=VMEM_LIMIT),
        name="conv_module",
    )(cacg, w, b, g, beta)


def _key_to_float(key):
    bits = key ^ ((key >> 31) & jnp.int32(0x7FFFFFFF))
    return lax.bitcast_convert_type(bits, F32)


def _float_to_key(v):
    bits = lax.bitcast_convert_type(v, jnp.int32)
    return bits ^ ((bits >> 31) & jnp.int32(0x7FFFFFFF))


KEY_NEG_INF = int(np.int32(np.uint32(0xFF800000)) ^ np.int32(0x7FFFFFFF))


def _dsa_kernel(far_ref, q_ref, k_ref, vt_ref, iq_ref, ik_ref, misc_ref, nb_ref, o_ref,
                sc_ref, qm_ref, acc_ref, thr_ref, *, topk):
    i = pl.program_id(1)
    s_len = k_ref.shape[1]
    q0 = i * QB
    lane = lax.broadcasted_iota(jnp.int32, (LANES, QB), 1)
    row = lax.broadcasted_iota(jnp.int32, (LANES, QB), 0)
    left = lax.broadcasted_iota(jnp.int32, (QB, LANES), 1) < HEAD_DIM
    qpos = q0 + lane

    iq = iq_ref[0]
    misc_t = misc_ref[0].T
    iqm, iw = [], []
    for h in range(N_IDX_HEADS):
        pair = iq[:, (h // 2) * LANES:(h // 2 + 1) * LANES]
        iqm.append(jnp.where(left if h % 2 == 0 else ~left, pair, jnp.zeros_like(pair)))
        iw.append(misc_t[MISC_IW + h:MISC_IW + h + 1, :] * (IDX_DIM ** -0.5 * N_IDX_HEADS ** -0.5))

    def score_chunk(c, carry):
        k0 = pl.multiple_of(c * KC, KC)
        ikc = ik_ref[0, pl.ds(k0, KC), :]
        s = jnp.zeros((KC, QB), F32)
        for h in range(N_IDX_HEADS):
            s = s + jnp.maximum(_dot_nt(ikc, iqm[h]), 0.0) * iw[h]
        for t in range(KC // LANES):
            kpos = k0 + t * LANES + row
            sc_ref[c * (KC // LANES) + t] = jnp.where(
                kpos <= qpos, s[t * LANES:(t + 1) * LANES, :], -jnp.inf)
        return carry

    per_chunk = KC // LANES
    n_chunks = (q0 + QB - 1) // KC + 1
    lax.fori_loop(0, n_chunks, score_chunk, 0)

    def count_blocks(pred):
        def body(c, acc):
            for t in range(per_chunk):
                bk = c * per_chunk + t
                acc = acc + jnp.where(pred(sc_ref[bk], bk), 1.0, 0.0)
            return acc
        acc = lax.fori_loop(0, n_chunks, body, jnp.zeros((LANES, QB), F32))
        return jnp.sum(acc, axis=0, keepdims=True)

    def fold(x, op):
        return op(x.reshape(LANES // SUBLANES, SUBLANES, QB), axis=0)

    def extremes(bk, carry):
        mx, mn, nf = carry
        s = sc_ref[bk]
        finite = s > -jnp.inf
        mx = jnp.maximum(mx, fold(s, jnp.max))
        mn = jnp.minimum(mn, fold(jnp.where(finite, s, jnp.inf), jnp.min))
        nf = nf + fold(jnp.where(finite, 1.0, 0.0), jnp.sum)
        return mx, mn, nf

    mx, mn, nf = lax.fori_loop(
        0, n_chunks * per_chunk, extremes,
        (jnp.full((SUBLANES, QB), -jnp.inf, F32), jnp.full((SUBLANES, QB), jnp.inf, F32),
         jnp.zeros((SUBLANES, QB), F32)))
    vmax = jnp.max(mx, axis=0, keepdims=True)
    vmin = jnp.min(mn, axis=0, keepdims=True)
    n_fin = jnp.sum(nf, axis=0, keepdims=True)
    vmax_b = jnp.broadcast_to(vmax, (LANES, QB))
    c_max = count_blocks(lambda s, bk: s >= vmax_b)

    everything = jnp.float32(2 * s_len)
    short = n_fin < topk
    top_tied = c_max >= topk
    lo0 = _float_to_key(vmin)
    hi0 = _float_to_key(vmax)
    closed0 = lo0 + 1 >= hi0
    done0 = short | top_tied | closed0
    thr0 = jnp.where(short, KEY_NEG_INF, jnp.where(top_tied, hi0, lo0))
    n_ge0 = jnp.where(short, n_fin, jnp.where(top_tied, c_max, n_fin))
    need0 = jnp.where(short, everything, jnp.where(top_tied, float(topk), topk - c_max))

    steps_per_check = 4

    def search_cond(state):
        it, _, _, _, _, done, _, _, _ = state
        return (it < 36) & (jnp.min(done) == 0)

    def search_steps(state):
        return lax.fori_loop(0, steps_per_check, lambda _, st: search_step(st), state)

    def search_step(state):
        it, lo, hi, c_lo, c_hi, done, thr_key, n_ge, need = state
        mid = (lo >> 1) + (hi >> 1) + (lo & hi & 1)
        mid_b = jnp.broadcast_to(_key_to_float(mid), (LANES, QB))
        cnt = count_blocks(lambda s, bk: s >= mid_b)
        ok = cnt >= topk
        lo2, c_lo2 = jnp.where(ok, mid, lo), jnp.where(ok, cnt, c_lo)
        hi2, c_hi2 = jnp.where(ok, hi, mid), jnp.where(ok, c_hi, cnt)
        hit = cnt == topk
        newly = (done == 0) & (hit | (lo2 + 1 == hi2))
        thr_key = jnp.where(newly, jnp.where(hit, mid, lo2), thr_key)
        n_ge = jnp.where(newly, jnp.where(hit, float(topk), c_lo2), n_ge)
        need = jnp.where(newly, jnp.where(hit, everything, topk - c_hi2), need)
        frozen = done == 1
        return (it + 1, jnp.where(frozen, lo, lo2), jnp.where(frozen, hi, hi2),
                jnp.where(frozen, c_lo, c_lo2), jnp.where(frozen, c_hi, c_hi2),
                jnp.where(newly, 1, done), thr_key, n_ge, need)

    state = lax.while_loop(
        search_cond, search_steps,
        (jnp.int32(0), lo0, hi0, n_fin, c_max, done0.astype(jnp.int32), thr0, n_ge0, need0))
    thr = _key_to_float(state[6])
    n_ge, need = state[7], state[8]
    thr_b = jnp.broadcast_to(thr, (LANES, QB))
    thr_ref[...] = thr_b

    @pl.when(jnp.max(n_ge) > topk)
    def _():
        rank_mat = jnp.where(
            lax.broadcasted_iota(jnp.int32, (LANES, LANES), 0)
            >= lax.broadcasted_iota(jnp.int32, (LANES, LANES), 1), 1.0, 0.0).astype(BF16)

        def demote(c, seen):
            for t in range(per_chunk):
                bk = c * per_chunk + t
                s = sc_ref[bk]
                tie = s == thr_b
                ind = jnp.where(tie, 1.0, 0.0)
                rank = seen + _dot(rank_mat, ind.astype(BF16))
                sc_ref[bk] = jnp.where(tie & (rank > need), -jnp.inf, s)
                seen = seen + jnp.sum(ind, axis=0, keepdims=True)
            return seen

        lax.fori_loop(0, n_chunks, demote, jnp.zeros((1, QB), F32))

    q = q_ref[0]
    for pr in range(N_HEADS // 2):
        pair = q[:, pr * LANES:(pr + 1) * LANES]
        zero = jnp.zeros_like(pair)
        qm_ref[pr] = jnp.concatenate([jnp.where(left, pair, zero), jnp.where(left, zero, pair)], axis=0)
    acc_ref[...] = jnp.zeros(acc_ref.shape, F32)
    odd = lax.broadcasted_iota(jnp.int32, (1, 2 * QB), 1) >= QB

    def mask_tile(bk, active, causal):
        t = jnp.where(active, thr_ref[...], jnp.inf)
        sel = sc_ref[bk] >= t
        if causal:
            sel = sel & (bk * LANES + row <= qpos)
        return jnp.where(sel, 0.0, NEG)

    def attend(blocks, maskadd, near, stats):
        width = len(blocks) * LANES
        k0 = pl.multiple_of(blocks[0] * LANES, LANES)
        maskadd2 = jnp.concatenate([maskadd, maskadd], axis=1)
        new_stats = []
        for pr in range(N_HEADS // 2):
            kp = k_ref[0, pl.ds(k0, width), pr * LANES:(pr + 1) * LANES]
            lg = _dot_nt(kp, qm_ref[pr]) + maskadd2
            if near is None:
                shift = jnp.where(odd, far_ref[2 * pr + 1], far_ref[2 * pr])
            else:
                lg = lg + jnp.concatenate(
                    [nb_ref[2 * pr + hh, near:near + width, :] for hh in range(2)], axis=1)
                shift = 0.0
            m_old, l_old = stats[pr]
            m_new = jnp.maximum(m_old, jnp.max(lg, axis=0, keepdims=True) + shift)
            alpha = jnp.exp(m_old - m_new)
            p = jnp.exp(lg - (m_new - shift))
            new_stats.append((m_new, alpha * l_old + jnp.sum(p, axis=0, keepdims=True)))
            rows = slice(pr * LANES, (pr + 1) * LANES)
            vt = jnp.concatenate([vt_ref[0, bk, rows, :] for bk in blocks], axis=1)
            pv = _dot(vt, p.astype(BF16))
            for hh in range(2):
                hs = slice(pr * LANES + hh * HEAD_DIM, pr * LANES + (hh + 1) * HEAD_DIM)
                cs = slice(hh * QB, (hh + 1) * QB)
                acc_ref[hs, :] = (alpha[:, cs] * acc_ref[hs, :]
                                  + pv[hh * HEAD_DIM:(hh + 1) * HEAD_DIM, cs])
        return new_stats

    stats = [(jnp.full((1, 2 * QB), NEG, F32), jnp.zeros((1, 2 * QB), F32))
             for _ in range(N_HEADS // 2)]

    far_end = jnp.maximum(q0 - LANES, 0)

    def far_chunk(c, stats):
        blocks = [c * per_chunk + t for t in range(per_chunk)]
        maskadd = jnp.concatenate(
            [mask_tile(bk, bk * LANES < far_end, False) for bk in blocks], axis=0)
        return attend(blocks, maskadd, None, stats)

    stats = lax.fori_loop(0, (far_end + KC - 1) // KC, far_chunk, stats)

    diag = [q0 // LANES + t for t in range(QB // LANES)]
    prev = jnp.maximum(q0 // LANES - 1, 0)
    stats = attend([prev], mask_tile(prev, i >= 1, False), 0, stats)
    stats = attend(diag, jnp.concatenate([mask_tile(bk, True, True) for bk in diag], axis=0),
                   LANES, stats)

    out_t = jnp.concatenate(
        [acc_ref[h * HEAD_DIM:(h + 1) * HEAD_DIM, :] / stats[h // 2][1][:, (h % 2) * QB:(h % 2 + 1) * QB]
         for h in range(N_HEADS)],
        axis=0)
    o_ref[0] = out_t.T.astype(BF16)


def _dsa(far_bias, q, k, vt, iq, ik2, misc, near_bias):
    bsz, s, _ = q.shape
    topk = min(TOPK_MAX, s // 4)
    nq = s // QB
    blk = lambda w: pl.BlockSpec((1, QB, w), lambda bi, i: (bi, i, 0))
    once = pl.Buffered(1)
    full = lambda w: pl.BlockSpec((1, s, w), lambda bi, i: (bi, 0, 0), pipeline_mode=once)
    return pl.pallas_call(
        functools.partial(_dsa_kernel, topk=topk),
        grid=(bsz, nq),
        in_specs=[pl.BlockSpec(memory_space=pltpu.SMEM),
                  blk(ATT_W), full(ATT_W),
                  pl.BlockSpec((1, s // LANES, ATT_W, LANES), lambda bi, i: (bi, 0, 0, 0),
                               pipeline_mode=once),
                  blk(N_IDX_HEADS * IDX_DIM), full(LANES), blk(LANES),
                  pl.BlockSpec((N_HEADS, LANES + QB, QB), lambda bi, i: (0, 0, 0), pipeline_mode=once)],
        out_specs=blk(ATT_W),
        out_shape=jax.ShapeDtypeStruct((bsz, s, ATT_W), BF16),
        scratch_shapes=[
            pltpu.VMEM((_round_up(s, KC) // LANES, LANES, QB), F32),
            pltpu.VMEM((N_HEADS // 2, 2 * QB, LANES), BF16),
            pltpu.VMEM((ATT_W, QB), F32),
            pltpu.VMEM((LANES, QB), F32),
        ],
        compiler_params=pltpu.CompilerParams(
            dimension_semantics=("arbitrary", "arbitrary"), vmem_limit_bytes=VMEM_LIMIT),
        name="dsa",
    )(far_bias, q, k, vt, iq, ik2, misc, near_bias)


def _round_up(a, b):
    return (a + b - 1) // b * b


def _ssd_kernel(xbc_ref, z_ref, misc_ref, cw_ref, cb_ref, dtb_ref, alog_ref, dsk_ref, gn_ref,
                o_ref, ext_ref, st_ref):
    ts = o_ref.shape[1]
    j = pl.program_id(1)

    @pl.when(j == 0)
    def _():
        ext_ref[0:HALO, :] = jnp.zeros((HALO, SSM_XBC), F32)
        st_ref[...] = jnp.zeros(st_ref.shape, F32)

    @pl.when(j > 0)
    def _():
        ext_ref[0:HALO, :] = ext_ref[ts:ts + HALO, :]

    ext_ref[HALO:, :] = xbc_ref[0].astype(F32)

    lane = lax.broadcasted_iota(jnp.int32, (CHUNK, LANES), 1)
    row = lax.broadcasted_iota(jnp.int32, (CHUNK, LANES), 0)
    tril = row >= lane
    tril_b = jnp.where(tril, 1.0, 0.0).astype(BF16)
    is_dt = (lane >= MISC_DT) & (lane < MISC_DT + SSM_HEADS)
    e_row = lax.broadcasted_iota(jnp.int32, (LANES, SSM_INNER), 0)
    e_col = lax.broadcasted_iota(jnp.int32, (LANES, SSM_INNER), 1)
    expand = jnp.where(e_row - MISC_DT == e_col // SSM_HEAD_DIM, 1.0, 0.0).astype(BF16)
    left = lane < SSM_HEAD_DIM
    lane1 = lax.broadcasted_iota(jnp.int32, (1, LANES), 1)
    a_row = jnp.where((lane1 >= MISC_DT) & (lane1 < MISC_DT + SSM_HEADS), -jnp.exp(alog_ref[...]), 0.0)
    cw = cw_ref[...]
    off = HALO - (SSM_CONV_K - 1)
    gw = SSM_GROUPS * SSM_STATE

    for c in range(ts // CHUNK):
        r0 = c * CHUNK
        conv = jnp.broadcast_to(cb_ref[...], (CHUNK, SSM_XBC))
        for tap in range(SSM_CONV_K):
            conv = conv + cw[tap:tap + 1, :] * ext_ref[r0 + off + tap:r0 + off + tap + CHUNK, :]
        xc = _silu(conv)
        xs = xc[:, 0:SSM_INNER]

        dt = jnp.where(is_dt, jax.nn.softplus(misc_ref[0, r0:r0 + CHUNK, :] + dtb_ref[...]), 0.0)
        cs = _cumsum_rows(tril_b, dt * a_row)
        cs_last = cs[CHUNK - 1:CHUNK, :]
        dt_e = _dot_split(dt, expand)
        dec_e = _dot_split(jnp.exp(cs_last - cs), expand)
        ecs_e = _dot_split(jnp.exp(cs), expand)
        cs_t = cs.T

        xd = xs * dt_e
        xd_b = xd.astype(BF16)
        xdd_b = (xd * dec_e).astype(BF16)
        y = dsk_ref[...] * xs
        st_old = st_ref[...]
        st_b = st_old.astype(BF16)
        new_states = []
        y_parts = []
        for g in range(SSM_GROUPS):
            bm = xc[:, SSM_INNER + g * SSM_STATE:SSM_INNER + (g + 1) * SSM_STATE]
            cm_b = xc[:, SSM_INNER + gw + g * SSM_STATE:SSM_INNER + gw + (g + 1) * SSM_STATE].astype(BF16)
            cb = _dot_nt(cm_b, bm.astype(BF16))
            hpg = SSM_HEADS // SSM_GROUPS
            gl = slice(g * hpg * SSM_HEAD_DIM, (g + 1) * hpg * SSM_HEAD_DIM)
            y_off = _dot(cm_b, st_b[:, gl]) * ecs_e[:, gl]
            new_states.append(_dot(bm.T.astype(BF16), xdd_b[:, gl]))
            diag = []
            for pr in range(hpg // 2):
                xp = xd_b[:, gl][:, pr * LANES:(pr + 1) * LANES]
                res = []
                for hh in range(2):
                    hd = g * hpg + 2 * pr + hh
                    seg = cs[:, MISC_DT + hd:MISC_DT + hd + 1] - cs_t[MISC_DT + hd:MISC_DT + hd + 1, :]
                    decay = jnp.exp(jnp.where(tril, seg, -jnp.inf))
                    res.append(_dot((cb * decay).astype(BF16), xp))
                diag.append(jnp.where(left, res[0], res[1]))
            y_parts.append(jnp.concatenate(diag, axis=-1) + y_off)
        y = y + jnp.concatenate(y_parts, axis=-1)
        st_ref[...] = st_old * ecs_e[CHUNK - 1:CHUNK, :] + jnp.concatenate(new_states, axis=-1)

        yz = y * _silu(z_ref[0, r0:r0 + CHUNK, :].astype(F32))
        yn = yz * lax.rsqrt(jnp.mean(yz * yz, axis=-1, keepdims=True) + EPS) * gn_ref[...]
        o_ref[0, r0:r0 + CHUNK, :] = yn.astype(BF16)


def _cumsum_rows(tril_b, x):
    hi = x.astype(BF16)
    r1 = x - hi.astype(F32)
    mid = r1.astype(BF16)
    lo = (r1 - mid.astype(F32)).astype(BF16)
    return _dot(tril_b, hi) + _dot(tril_b, mid) + _dot(tril_b, lo)


def _ssd(xbc, z, misc, cw, cb, dtb, alog, dsk, gn, ts):
    bsz, s, _ = xbc.shape
    blk = lambda w: pl.BlockSpec((1, ts, w), lambda bi, j: (bi, j, 0))
    return pl.pallas_call(
        _ssd_kernel,
        grid=(bsz, s // ts),
        in_specs=[blk(SSM_XBC), blk(SSM_INNER), blk(LANES),
                  _const_spec((HALO, SSM_XBC)), _const_spec((1, SSM_XBC)),
                  _const_spec((1, LANES)), _const_spec((1, LANES)),
                  _const_spec((1, SSM_INNER)), _const_spec((1, SSM_INNER))],
        out_specs=blk(SSM_INNER),
        out_shape=jax.ShapeDtypeStruct((bsz, s, SSM_INNER), BF16),
        scratch_shapes=[pltpu.VMEM((ts + HALO, SSM_XBC), F32),
                        pltpu.VMEM((SSM_STATE, SSM_INNER), F32)],
        compiler_params=pltpu.CompilerParams(
            dimension_semantics=("arbitrary", "arbitrary"), vmem_limit_bytes=VMEM_LIMIT),
        name="ssd",
    )(xbc, z, misc, cw, cb, dtb, alog, dsk, gn)


FF_CHUNK = D_FF // 2


def _ffn_kernel(x_ref, u_ref, att_ref, y_ref, wo_ref, g_ref, wup_ref, cw_ref, cb_ref, wdn_ref,
                o_ref, ext_a, ext_g, carry_ref, gated_ref):
    tm = x_ref.shape[1]
    j = pl.program_id(1)

    @pl.when(j == 0)
    def _():
        carry_ref[...] = jnp.zeros(carry_ref.shape, F32)

    x1 = (x_ref[0]
          + _dot(u_ref[0], wo_ref[0:CONV_CH, :])
          + _dot(att_ref[0], wo_ref[CONV_CH:CONV_CH + ATT_W, :])
          + _dot(y_ref[0], wo_ref[CONV_CH + ATT_W:D_MIX, :]))
    hb = (x1 * lax.rsqrt(jnp.mean(x1 * x1, axis=-1, keepdims=True) + EPS) * g_ref[...]).astype(BF16)

    off = HALO - (FFN_CONV_K - 1)

    def conv_half(ext, c0):
        ext[0:HALO, :] = carry_ref[:, c0:c0 + FF_CHUNK]
        ext[HALO:, :] = _dot(hb, wup_ref[:, c0:c0 + FF_CHUNK])
        carry_ref[:, c0:c0 + FF_CHUNK] = ext[tm:tm + HALO, :]
        out = jnp.broadcast_to(cb_ref[:, c0:c0 + FF_CHUNK], (tm, FF_CHUNK))
        for tap in range(FFN_CONV_K):
            out = out + cw_ref[tap:tap + 1, c0:c0 + FF_CHUNK] * ext[off + tap:off + tap + tm, :]
        return out

    for c in range(D_FF // FF_CHUNK):
        fa = conv_half(ext_a, c * FF_CHUNK)
        fg = conv_half(ext_g, D_FF + c * FF_CHUNK)
        gated_ref[:, c * FF_CHUNK:(c + 1) * FF_CHUNK] = (_silu(fg) * fa).astype(BF16)
    o_ref[0] = x1 + _dot(gated_ref[...], wdn_ref[...])


def _ffn(x, u, att, y, wo, g, wup, cw, cb, wdn, tm):
    bsz, s, d = x.shape
    blk = lambda w: pl.BlockSpec((1, tm, w), lambda bi, j: (bi, j, 0))
    single = lambda shape: pl.BlockSpec(shape, lambda *_: (0,) * len(shape),
                                        pipeline_mode=pl.Buffered(1))
    return pl.pallas_call(
        _ffn_kernel,
        grid=(bsz, s // tm),
        in_specs=[blk(d), blk(CONV_CH), blk(ATT_W), blk(SSM_INNER),
                  single((D_MIX, d)), _const_spec((1, d)),
                  single((d, 2 * D_FF)), _const_spec((HALO, 2 * D_FF)), _const_spec((1, 2 * D_FF)),
                  single((D_FF, d))],
        out_specs=blk(d),
        out_shape=jax.ShapeDtypeStruct((bsz, s, d), F32),
        scratch_shapes=[pltpu.VMEM((tm + HALO, FF_CHUNK), F32),
                        pltpu.VMEM((tm + HALO, FF_CHUNK), F32),
                        pltpu.VMEM((HALO, 2 * D_FF), F32),
                        pltpu.VMEM((tm, D_FF), BF16)],
        compiler_params=pltpu.CompilerParams(
            dimension_semantics=("arbitrary", "arbitrary"), vmem_limit_bytes=VMEM_LIMIT),
        name="ffn",
    )(x, u, att, y, wo, g, wup, cw, cb, wdn)


def _t5_bucket(rel):
    n = jnp.maximum(rel, 0)
    max_exact = NUM_BUCKETS // 2
    nf = jnp.maximum(n, 1).astype(F32)
    large = max_exact + (jnp.log(nf / max_exact) / math.log(MAX_DISTANCE / max_exact)
                         * (NUM_BUCKETS - max_exact)).astype(jnp.int32)
    large = jnp.minimum(large, NUM_BUCKETS - 1)
    return jnp.where(n < max_exact, n, large)


def _bias_tables(rel_bias):
    kk = jnp.arange(LANES + QB)[:, None]
    tq = jnp.arange(QB)[None, :]
    rel = tq + LANES - kk
    near = jnp.transpose(rel_bias[_t5_bucket(rel)].astype(F32), (2, 0, 1))
    far = rel_bias[_t5_bucket(jnp.int32(LANES + 1))].astype(F32)
    return near, far


def _pad_rows(w, rows):
    return jnp.concatenate([w, jnp.zeros((w.shape[0], rows - w.shape[1], w.shape[2]), w.dtype)], axis=1)


def _pad_lanes(v, start, width=LANES):
    out = jnp.zeros(v.shape[:-1] + (width,), v.dtype)
    return out.at[..., start:start + v.shape[-1]].set(v)


def kernel(x, rel_bias, norm_mix_g, w_in, conv_dw_w, conv_dw_b, conv_ln_g, conv_ln_b, q_norm_g, k_norm_g, ssm_conv_w, ssm_conv_b, dt_bias, a_log, d_skip, ssm_norm_g, w_out, norm_ffn_g, w_up, ffn_conv_w, ffn_conv_b, w_down):
    bsz, s, d = x.shape
    depth = w_in.shape[0]
    assert s % KC == 0 and s >= 2 * QB

    sizes = [CONV_CH, CONV_CH, ATT_W, ATT_W, ATT_W, N_IDX_HEADS * IDX_DIM, IDX_DIM, N_IDX_HEADS,
             SSM_INNER, SSM_XBC, SSM_HEADS]
    splits = [int(v) for v in np.cumsum(sizes)[:-1]]
    ca, cg, wq, wk, wv, wiq, wik, wiw, wz, wxbc, wdt = jnp.split(w_in, splits, axis=-1)
    w_main = jnp.concatenate([ca, cg, wq, wk, wiq, wik, wik, wz, wxbc], axis=-1).astype(BF16)
    w_vt = jnp.swapaxes(wv, 1, 2).astype(BF16)
    w_misc = jnp.concatenate(
        [_pad_lanes(wiw, 0, MISC_DT), _pad_lanes(wdt, 0, LANES - MISC_DT)], axis=-1).astype(BF16)
    seg = jnp.asarray(np.kron(np.eye(N_HEADS), np.full((HEAD_DIM, HEAD_DIM), 1.0 / HEAD_DIM)), BF16)

    near_bias, far_bias = _bias_tables(rel_bias)

    layer_params = dict(
        g_mix=norm_mix_g[:, None, :], w_main=w_main, w_misc=w_misc, w_vt=w_vt,
        gq=jnp.tile(q_norm_g, (1, N_HEADS))[:, None, :], gk=jnp.tile(k_norm_g, (1, N_HEADS))[:, None, :],
        conv_w=_pad_rows(conv_dw_w, CONV_HALO), conv_b=conv_dw_b[:, None, :],
        ln_g=conv_ln_g[:, None, :], ln_b=conv_ln_b[:, None, :],
        ssm_w=_pad_rows(ssm_conv_w, HALO), ssm_b=ssm_conv_b[:, None, :],
        dtb=_pad_lanes(dt_bias, MISC_DT)[:, None, :], alog=_pad_lanes(a_log, MISC_DT)[:, None, :],
        dsk=jnp.repeat(d_skip, SSM_HEAD_DIM, axis=-1)[:, None, :], gn=ssm_norm_g[:, None, :],
        w_out=w_out.astype(BF16), g_ffn=norm_ffn_g[:, None, :], w_up=w_up.astype(BF16),
        ffn_w=_pad_rows(ffn_conv_w, HALO), ffn_b=ffn_conv_b[:, None, :], w_down=w_down.astype(BF16),
    )

    tm = min(512, s)

    def layer(xc, p):
        (cacg, q, k, vt, iq, ik2, z, xbc, misc) = _in_proj(
            xc.reshape(bsz * s, d), p["g_mix"], p["w_main"], p["w_misc"], p["w_vt"], p["gq"], p["gk"],
            seg, tm)
        r3 = lambda a: a.reshape(bsz, s, a.shape[-1])
        u = _conv_module(r3(cacg), p["conv_w"], p["conv_b"], p["ln_g"], p["ln_b"], tm)
        att = _dsa(far_bias, r3(q), r3(k), vt.reshape(bsz, s // LANES, ATT_W, LANES), r3(iq), r3(ik2),
                   r3(misc), near_bias)
        y = _ssd(r3(xbc), r3(z), r3(misc), p["ssm_w"], p["ssm_b"], p["dtb"], p["alog"],
                 p["dsk"], p["gn"], tm)
        xn = _ffn(xc, u, att, y, p["w_out"], p["g_ffn"], p["w_up"], p["ffn_w"], p["ffn_b"],
                  p["w_down"], tm)
        return xn, None

    out, _ = lax.scan(layer, x, layer_params)
    return out
```

```python
import functools
import math

import jax
import jax.numpy as jnp
import numpy as np
from jax import lax
from jax.experimental import pallas as pl
from jax.experimental.pallas import tpu as pltpu

F32 = jnp.float32
BF16 = jnp.bfloat16

EPS = 1e-6
LANES = 128
SUBLANES = 8
CONV_CH = 512
CONV_K = 31
N_HEADS = 8
HEAD_DIM = 64
ATT_W = N_HEADS * HEAD_DIM
N_IDX_HEADS = 4
IDX_DIM = 64
TOPK_MAX = 256
NUM_BUCKETS = 32
MAX_DISTANCE = 128
SSM_HEADS = 8
SSM_HEAD_DIM = 64
SSM_INNER = SSM_HEADS * SSM_HEAD_DIM
SSM_GROUPS = 2
SSM_STATE = 128
SSM_CONV_K = 4
SSM_XBC = SSM_INNER + 2 * SSM_GROUPS * SSM_STATE
CHUNK = 128
D_MIX = CONV_CH + ATT_W + SSM_INNER
D_FF = 2816
FFN_CONV_K = 3

MISC_IW = 0
MISC_DT = 8
HALO = 8
CONV_HALO = 32
QB = 256
KC = 512
NEG = -1e30
DENOM_FLOOR = 1e-25
BOUND_SLACK = 1.03
VMEM_LIMIT = 56 * 1024 * 1024

NT_DIMS = (((1,), (1,)), ((), ()))


def _dot(a, b):
    return jnp.dot(a, b, preferred_element_type=F32)


def _dot_nt(a, b):
    return lax.dot_general(a, b, NT_DIMS, preferred_element_type=F32)


def _dot_split(x, m_bf16):
    hi = x.astype(BF16)
    r1 = x - hi.astype(F32)
    mid = r1.astype(BF16)
    lo = (r1 - mid.astype(F32)).astype(BF16)
    return _dot(hi, m_bf16) + _dot(mid, m_bf16) + _dot(lo, m_bf16)


def _silu(x):
    return x * jax.nn.sigmoid(x)


def _const_spec(shape):
    n = len(shape)
    return pl.BlockSpec(shape, lambda *_: (0,) * n)


def _in_proj_kernel(x_ref, g_ref, w_ref, wm_ref, wvt_ref, gq_ref, gk_ref, seg_ref,
                    cacg_ref, q_ref, k_ref, vt_ref, iq_ref, ik_ref, z_ref, xbc_ref, misc_ref):
    x = x_ref[...]
    h = x * lax.rsqrt(jnp.mean(x * x, axis=-1, keepdims=True) + EPS) * g_ref[...]
    hb = h.astype(BF16)

    def proj(c0, width):
        return _dot(hb, w_ref[:, c0:c0 + width])

    cacg_ref[...] = proj(0, 1024).astype(BF16)

    def qk_norm(c0, gain_ref, scale):
        t = proj(c0, ATT_W)
        ms = _dot_split(t * t, seg_ref[...])
        return (t * lax.rsqrt(ms + EPS) * (gain_ref[...] * scale)).astype(BF16)

    q_ref[...] = qk_norm(1024, gq_ref, HEAD_DIM ** -0.5)
    k_ref[...] = qk_norm(1536, gk_ref, 1.0)
    vt = _dot_nt(wvt_ref[...], hb).astype(BF16)
    for j in range(vt_ref.shape[0]):
        vt_ref[j] = vt[:, j * LANES:(j + 1) * LANES]
    iq_ref[...] = proj(2048, N_IDX_HEADS * IDX_DIM).astype(BF16)
    ik_ref[...] = proj(2304, LANES).astype(BF16)
    z_ref[...] = proj(2432, SSM_INNER).astype(BF16)
    xbc_ref[...] = proj(2944, SSM_XBC).astype(BF16)
    misc_ref[...] = _dot(hb, wm_ref[...])


W_MAIN_COLS = 2944 + SSM_XBC


def _in_proj(x2, g, w_main, w_misc, w_vt, gq, gk, seg, tm):
    t, d = x2.shape
    rows = lambda w, dt: (pl.BlockSpec((tm, w), lambda i: (i, 0)), jax.ShapeDtypeStruct((t, w), dt))
    vt_out = (pl.BlockSpec((tm // LANES, ATT_W, LANES), lambda i: (i, 0, 0)),
              jax.ShapeDtypeStruct((t // LANES, ATT_W, LANES), BF16))
    outs = [rows(1024, BF16), rows(ATT_W, BF16), rows(ATT_W, BF16), vt_out,
            rows(N_IDX_HEADS * IDX_DIM, BF16), rows(LANES, BF16), rows(SSM_INNER, BF16),
            rows(SSM_XBC, BF16), rows(LANES, F32)]
    return pl.pallas_call(
        _in_proj_kernel,
        grid=(t // tm,),
        in_specs=[pl.BlockSpec((tm, d), lambda i: (i, 0)),
                  _const_spec((1, d)),
                  _const_spec((d, W_MAIN_COLS)),
                  _const_spec((d, LANES)),
                  _const_spec((ATT_W, d)),
                  _const_spec((1, ATT_W)), _const_spec((1, ATT_W)),
                  _const_spec((ATT_W, ATT_W))],
        out_specs=[spec for spec, _ in outs],
        out_shape=[shape for _, shape in outs],
        compiler_params=pltpu.CompilerParams(
            dimension_semantics=("arbitrary",), vmem_limit_bytes=VMEM_LIMIT),
        name="in_proj",
    )(x2, g, w_main, w_misc, w_vt, gq, gk, seg)


CONV_ROWS = 64


def _conv_kernel(cacg_ref, w_ref, b_ref, g_ref, beta_ref, o_ref, ext_ref):
    ts = o_ref.shape[1]
    j = pl.program_id(1)

    @pl.when(j == 0)
    def _():
        ext_ref[0:CONV_HALO, :] = jnp.zeros((CONV_HALO, CONV_CH), F32)

    @pl.when(j > 0)
    def _():
        ext_ref[0:CONV_HALO, :] = ext_ref[ts:ts + CONV_HALO, :]

    ca = cacg_ref[0, :, 0:CONV_CH].astype(F32)
    cg = cacg_ref[0, :, CONV_CH:2 * CONV_CH].astype(F32)
    ext_ref[CONV_HALO:, :] = ca * jax.nn.sigmoid(cg)

    w = w_ref[...]
    off = CONV_HALO - (CONV_K - 1)
    for r in range(ts // CONV_ROWS):
        base = r * CONV_ROWS
        acc = jnp.broadcast_to(b_ref[...], (CONV_ROWS, CONV_CH))
        for res in range(SUBLANES):
            rows = CONV_ROWS + (SUBLANES if res else 0)
            part = None
            for tap in range(CONV_K):
                if (off + tap) % SUBLANES != res:
                    continue
                start = base + off + tap - res
                term = w[tap:tap + 1, :] * ext_ref[start:start + rows, :]
                part = term if part is None else part + term
            acc = acc + part[res:res + CONV_ROWS, :]
        mu = jnp.mean(acc, axis=-1, keepdims=True)
        cen = acc - mu
        var = jnp.mean(cen * cen, axis=-1, keepdims=True)
        yn = cen * lax.rsqrt(var + EPS) * g_ref[...] + beta_ref[...]
        o_ref[0, base:base + CONV_ROWS, :] = _silu(yn).astype(BF16)


def _conv_module(cacg, w, b, g, beta, ts):
    bsz, s, _ = cacg.shape
    return pl.pallas_call(
        _conv_kernel,
        grid=(bsz, s // ts),
        in_specs=[pl.BlockSpec((1, ts, 2 * CONV_CH), lambda bi, j: (bi, j, 0)),
                  _const_spec((CONV_HALO, CONV_CH)),
                  _const_spec((1, CONV_CH)), _const_spec((1, CONV_CH)), _const_spec((1, CONV_CH))],
        out_specs=pl.BlockSpec((1, ts, CONV_CH), lambda bi, j: (bi, j, 0)),
        out_shape=jax.ShapeDtypeStruct((bsz, s, CONV_CH), BF16),
        scratch_shapes=[pltpu.VMEM((ts + CONV_HALO, CONV_CH), F32)],
        compiler_params=pltpu.CompilerParams(
            dimension_semantics=("arbitrary", "arbitrary"), vmem_limit_bytes=VMEM_LIMIT),
        name="conv_module",
    )(cacg, w, b, g, beta)


def _key_to_float(key):
    bits = key ^ ((key >> 31) & jnp.int32(0x7FFFFFFF))
    return lax.bitcast_convert_type(bits, F32)


def _float_to_key(v):
    bits = lax.bitcast_convert_type(v, jnp.int32)
    return bits ^ ((bits >> 31) & jnp.int32(0x7FFFFFFF))


KEY_NEG_INF = int(np.int32(np.uint32(0xFF800000)) ^ np.int32(0x7FFFFFFF))


def _dsa_kernel(hb_ref, q_ref, k_ref, vt_ref, iq_ref, ik_ref, misc_ref, nb_ref, o_ref,
                sc_ref, qm_ref, acc_ref, thr_ref, l_ref, *, topk):
    i = pl.program_id(1)
    s_len = k_ref.shape[1]
    q0 = i * QB
    lane = lax.broadcasted_iota(jnp.int32, (LANES, QB), 1)
    row = lax.broadcasted_iota(jnp.int32, (LANES, QB), 0)
    left = lax.broadcasted_iota(jnp.int32, (QB, LANES), 1) < HEAD_DIM
    qpos = q0 + lane

    iq = iq_ref[0]
    misc_t = misc_ref[0].T
    iqm, iw = [], []
    for h in range(N_IDX_HEADS):
        pair = iq[:, (h // 2) * LANES:(h // 2 + 1) * LANES]
        iqm.append(jnp.where(left if h % 2 == 0 else ~left, pair, jnp.zeros_like(pair)))
        iw.append(misc_t[MISC_IW + h:MISC_IW + h + 1, :] * (IDX_DIM ** -0.5 * N_IDX_HEADS ** -0.5))

    def score_chunk(c, carry):
        k0 = pl.multiple_of(c * KC, KC)
        ikc = ik_ref[0, pl.ds(k0, KC), :]
        s = jnp.zeros((KC, QB), F32)
        for h in range(N_IDX_HEADS):
            s = s + jnp.maximum(_dot_nt(ikc, iqm[h]), 0.0) * iw[h]
        for t in range(KC // LANES):
            kpos = k0 + t * LANES + row
            sc_ref[c * (KC // LANES) + t] = jnp.where(
                kpos <= qpos, s[t * LANES:(t + 1) * LANES, :], -jnp.inf)
        return carry

    per_chunk = KC // LANES
    n_chunks = (q0 + QB - 1) // KC + 1
    lax.fori_loop(0, n_chunks, score_chunk, 0)

    def count_blocks(pred):
        def body(c, acc):
            for t in range(per_chunk):
                bk = c * per_chunk + t
                acc = acc + jnp.where(pred(sc_ref[bk], bk), 1.0, 0.0)
            return acc
        acc = lax.fori_loop(0, n_chunks, body, jnp.zeros((LANES, QB), F32))
        return jnp.sum(acc, axis=0, keepdims=True)

    def fold(x, op):
        return op(x.reshape(LANES // SUBLANES, SUBLANES, QB), axis=0)

    def extremes(bk, carry):
        mx, mn, nf = carry
        s = sc_ref[bk]
        finite = s > -jnp.inf
        mx = jnp.maximum(mx, fold(s, jnp.max))
        mn = jnp.minimum(mn, fold(jnp.where(finite, s, jnp.inf), jnp.min))
        nf = nf + fold(jnp.where(finite, 1.0, 0.0), jnp.sum)
        return mx, mn, nf

    mx, mn, nf = lax.fori_loop(
        0, n_chunks * per_chunk, extremes,
        (jnp.full((SUBLANES, QB), -jnp.inf, F32), jnp.full((SUBLANES, QB), jnp.inf, F32),
         jnp.zeros((SUBLANES, QB), F32)))
    vmax = jnp.max(mx, axis=0, keepdims=True)
    vmin = jnp.min(mn, axis=0, keepdims=True)
    n_fin = jnp.sum(nf, axis=0, keepdims=True)
    vmax_b = jnp.broadcast_to(vmax, (LANES, QB))
    c_max = count_blocks(lambda s, bk: s >= vmax_b)

    everything = jnp.float32(2 * s_len)
    short = n_fin < topk
    top_tied = c_max >= topk
    lo0 = _float_to_key(vmin)
    hi0 = _float_to_key(vmax)
    closed0 = lo0 + 1 >= hi0
    done0 = short | top_tied | closed0
    thr0 = jnp.where(short, KEY_NEG_INF, jnp.where(top_tied, hi0, lo0))
    n_ge0 = jnp.where(short, n_fin, jnp.where(top_tied, c_max, n_fin))
    need0 = jnp.where(short, everything, jnp.where(top_tied, float(topk), topk - c_max))

    steps_per_check = 4

    def search_cond(state):
        it, _, _, _, _, done, _, _, _ = state
        return (it < 36) & (jnp.min(done) == 0)

    def search_steps(state):
        return lax.fori_loop(0, steps_per_check, lambda _, st: search_step(st), state)

    def search_step(state):
        it, lo, hi, c_lo, c_hi, done, thr_key, n_ge, need = state
        mid = (lo >> 1) + (hi >> 1) + (lo & hi & 1)
        mid_b = jnp.broadcast_to(_key_to_float(mid), (LANES, QB))
        cnt = count_blocks(lambda s, bk: s >= mid_b)
        ok = cnt >= topk
        lo2, c_lo2 = jnp.where(ok, mid, lo), jnp.where(ok, cnt, c_lo)
        hi2, c_hi2 = jnp.where(ok, hi, mid), jnp.where(ok, c_hi, cnt)
        hit = cnt == topk
        newly = (done == 0) & (hit | (lo2 + 1 == hi2))
        thr_key = jnp.where(newly, jnp.where(hit, mid, lo2), thr_key)
        n_ge = jnp.where(newly, jnp.where(hit, float(topk), c_lo2), n_ge)
        need = jnp.where(newly, jnp.where(hit, everything, topk - c_hi2), need)
        frozen = done == 1
        return (it + 1, jnp.where(frozen, lo, lo2), jnp.where(frozen, hi, hi2),
                jnp.where(frozen, c_lo, c_lo2), jnp.where(frozen, c_hi, c_hi2),
                jnp.where(newly, 1, done), thr_key, n_ge, need)

    state = lax.while_loop(
        search_cond, search_steps,
        (jnp.int32(0), lo0, hi0, n_fin, c_max, done0.astype(jnp.int32), thr0, n_ge0, need0))
    thr = _key_to_float(state[6])
    n_ge, need = state[7], state[8]
    thr_b = jnp.broadcast_to(thr, (LANES, QB))
    thr_ref[...] = thr_b

    @pl.when(jnp.max(n_ge) > topk)
    def _():
        rank_mat = jnp.where(
            lax.broadcasted_iota(jnp.int32, (LANES, LANES), 0)
            >= lax.broadcasted_iota(jnp.int32, (LANES, LANES), 1), 1.0, 0.0).astype(BF16)

        def demote(c, seen):
            for t in range(per_chunk):
                bk = c * per_chunk + t
                s = sc_ref[bk]
                tie = s == thr_b
                ind = jnp.where(tie, 1.0, 0.0)
                rank = seen + _dot(rank_mat, ind.astype(BF16))
                sc_ref[bk] = jnp.where(tie & (rank > need), -jnp.inf, s)
                seen = seen + jnp.sum(ind, axis=0, keepdims=True)
            return seen

        lax.fori_loop(0, n_chunks, demote, jnp.zeros((1, QB), F32))

    q = q_ref[0]
    for pr in range(N_HEADS // 2):
        pair = q[:, pr * LANES:(pr + 1) * LANES]
        zero = jnp.zeros_like(pair)
        qm_ref[pr] = jnp.concatenate([jnp.where(left, pair, zero), jnp.where(left, zero, pair)], axis=0)
    odd =lax.broadcasted_iota(jnp.int32, (1, 2 * QB), 1) >= QB

    def mask_tile(bk, active, causal):
        t = jnp.where(active, thr_ref[...], jnp.inf)
        sel = sc_ref[bk] >= t
        if causal:
            sel = sel & (bk * LANES + row <= qpos)
        return jnp.where(sel, 0.0, NEG)

    def pair_row(values, pr):
        return jnp.where(odd, values[2 * pr + 1], values[2 * pr])

    def attend(blocks, maskadd, near, state, robust):
        width = len(blocks) * LANES
        k0 = pl.multiple_of(blocks[0] * LANES, LANES)
        maskadd2 = jnp.concatenate([maskadd, maskadd], axis=1)
        far = [hb_ref[0, h] for h in range(N_HEADS)]
        bound = [hb_ref[1, h] for h in range(N_HEADS)]
        new_state = []
        for pr in range(N_HEADS // 2):
            kp = k_ref[0, pl.ds(k0, width), pr * LANES:(pr + 1) * LANES]
            lg = _dot_nt(kp, qm_ref[pr]) + maskadd2
            if near is None:
                bias = pair_row(far, pr)
            else:
                lg = lg + jnp.concatenate(
                    [nb_ref[2 * pr + hh, near:near + width, :] for hh in range(2)], axis=1)
                bias = 0.0
            if robust:
                m_old, l_old = state[pr]
                m_new = jnp.maximum(m_old, jnp.max(lg, axis=0, keepdims=True) + bias)
                alpha = jnp.exp(m_old - m_new)
                p = jnp.exp(lg - (m_new - bias))
                new_state.append((m_new, alpha * l_old + jnp.sum(p, axis=0, keepdims=True)))
            else:
                alpha = None
                p = jnp.exp(lg + (bias - pair_row(bound, pr)))
                new_state.append(state[pr] + jnp.sum(p, axis=0, keepdims=True))
            rows = slice(pr * LANES, (pr + 1) * LANES)
            vt = jnp.concatenate([vt_ref[0, bk, rows, :] for bk in blocks], axis=1)
            pv = _dot(vt, p.astype(BF16))
            for hh in range(2):
                hs = slice(pr * LANES + hh * HEAD_DIM, pr * LANES + (hh + 1) * HEAD_DIM)
                cs = slice(hh * QB, (hh + 1) * QB)
                upd = pv[hh * HEAD_DIM:(hh + 1) * HEAD_DIM, cs]
                acc_ref[hs, :] = acc_ref[hs, :] + upd if alpha is None else alpha[:, cs] * acc_ref[hs, :] + upd
        return new_state

    far_end = jnp.maximum(q0 - LANES, 0)
    diag = [q0 // LANES + t for t in range(QB // LANES)]
    prev = jnp.maximum(q0 // LANES - 1, 0)

    def attention(robust):
        acc_ref[...] = jnp.zeros(acc_ref.shape, F32)
        zero = jnp.zeros((1, 2 * QB), F32)
        state = [(jnp.full((1, 2 * QB), NEG, F32), zero) if robust else zero
                 for _ in range(N_HEADS // 2)]

        def far_chunk(c, state):
            blocks = [c * per_chunk + t for t in range(per_chunk)]
            maskadd = jnp.concatenate(
                [mask_tile(bk, bk * LANES < far_end, False) for bk in blocks], axis=0)
            return attend(blocks, maskadd, None, state, robust)

        state = lax.fori_loop(0, (far_end + KC - 1) // KC, far_chunk, state)
        state = attend([prev], mask_tile(prev, i >= 1, False), 0, state, robust)
        state = attend(diag, jnp.concatenate([mask_tile(bk, True, True) for bk in diag], axis=0),
                       LANES, state, robust)
        for pr in range(N_HEADS // 2):
            l_ref[pr:pr + 1, :] = state[pr][1] if robust else state[pr]

    attention(robust=False)

    @pl.when(jnp.logical_not(jnp.min(l_ref[...]) > DENOM_FLOOR))
    def _():
        attention(robust=True)

    out_t = jnp.concatenate(
        [acc_ref[h * HEAD_DIM:(h + 1) * HEAD_DIM, :]
         / l_ref[h // 2:h // 2 + 1, (h % 2) * QB:(h % 2 + 1) * QB] for h in range(N_HEADS)],
        axis=0)
    o_ref[0] = out_t.T.astype(BF16)


def _dsa(head_scalars, q, k, vt, iq, ik2, misc, near_bias):
    bsz, s, _ = q.shape
    topk = min(TOPK_MAX, s // 4)
    nq = s // QB
    blk = lambda w: pl.BlockSpec((1, QB, w), lambda bi, i: (bi, i, 0))
    once = pl.Buffered(1)
    full = lambda w: pl.BlockSpec((1, s, w), lambda bi, i: (bi, 0, 0), pipeline_mode=once)
    return pl.pallas_call(
        functools.partial(_dsa_kernel, topk=topk),
        grid=(bsz, nq),
        in_specs=[pl.BlockSpec(memory_space=pltpu.SMEM),
                  blk(ATT_W), full(ATT_W),
                  pl.BlockSpec((1, s // LANES, ATT_W, LANES), lambda bi, i: (bi, 0, 0, 0),
                               pipeline_mode=once),
                  blk(N_IDX_HEADS * IDX_DIM), full(LANES), blk(LANES),
                  pl.BlockSpec((N_HEADS, LANES + QB, QB), lambda bi, i: (0, 0, 0), pipeline_mode=once)],
        out_specs=blk(ATT_W),
        out_shape=jax.ShapeDtypeStruct((bsz, s, ATT_W), BF16),
        scratch_shapes=[
            pltpu.VMEM((_round_up(s, KC) // LANES, LANES, QB), F32),
            pltpu.VMEM((N_HEADS // 2, 2 * QB, LANES), BF16),
            pltpu.VMEM((ATT_W, QB), F32),
            pltpu.VMEM((LANES, QB), F32),
            pltpu.VMEM((N_HEADS // 2, 2 * QB), F32),
        ],
        compiler_params=pltpu.CompilerParams(
            dimension_semantics=("arbitrary", "arbitrary"), vmem_limit_bytes=VMEM_LIMIT),
        name="dsa",
    )(head_scalars, q, k, vt, iq, ik2, misc, near_bias)


def _round_up(a, b):
    return (a + b - 1) // b * b


def _ssd_kernel(xbc_ref, z_ref, misc_ref, cw_ref, cb_ref, dtb_ref, alog_ref, dsk_ref, gn_ref,
                o_ref, ext_ref, st_ref):
    ts = o_ref.shape[1]
    j = pl.program_id(1)

    @pl.when(j == 0)
    def _():
        ext_ref[0:HALO, :] = jnp.zeros((HALO, SSM_XBC), F32)
        st_ref[...] = jnp.zeros(st_ref.shape, F32)

    @pl.when(j > 0)
    def _():
        ext_ref[0:HALO, :] = ext_ref[ts:ts + HALO, :]

    ext_ref[HALO:, :] = xbc_ref[0].astype(F32)

    lane = lax.broadcasted_iota(jnp.int32, (CHUNK, LANES), 1)
    row = lax.broadcasted_iota(jnp.int32, (CHUNK, LANES), 0)
    tril = row >= lane
    tril_b = jnp.where(tril, 1.0, 0.0).astype(BF16)
    is_dt = (lane >= MISC_DT) & (lane < MISC_DT + SSM_HEADS)
    e_row = lax.broadcasted_iota(jnp.int32, (LANES, SSM_INNER), 0)
    e_col = lax.broadcasted_iota(jnp.int32, (LANES, SSM_INNER), 1)
    expand = jnp.where(e_row - MISC_DT == e_col // SSM_HEAD_DIM, 1.0, 0.0).astype(BF16)
    left = lane < SSM_HEAD_DIM
    lane1 = lax.broadcasted_iota(jnp.int32, (1, LANES), 1)
    a_row = jnp.where((lane1 >= MISC_DT) & (lane1 < MISC_DT + SSM_HEADS), -jnp.exp(alog_ref[...]), 0.0)
    cw = cw_ref[...]
    off = HALO - (SSM_CONV_K - 1)
    gw = SSM_GROUPS * SSM_STATE

    for c in range(ts // CHUNK):
        r0 = c * CHUNK
        conv = jnp.broadcast_to(cb_ref[...], (CHUNK, SSM_XBC))
        for tap in range(SSM_CONV_K):
            conv = conv + cw[tap:tap + 1, :] * ext_ref[r0 + off + tap:r0 + off + tap + CHUNK, :]
        xc = _silu(conv)
        xs = xc[:, 0:SSM_INNER]

        dt = jnp.where(is_dt, jax.nn.softplus(misc_ref[0, r0:r0 + CHUNK, :] + dtb_ref[...]), 0.0)
        cs = _cumsum_rows(tril_b, dt * a_row)
        cs_last = cs[CHUNK - 1:CHUNK, :]
        dt_e = _dot_split(dt, expand)
        dec_e = _dot_split(jnp.exp(cs_last - cs), expand)
        ecs_e = _dot_split(jnp.exp(cs), expand)
        cs_t = cs.T

        xd = xs * dt_e
        xd_b = xd.astype(BF16)
        xdd_b = (xd * dec_e).astype(BF16)
        y = dsk_ref[...] * xs
        st_old = st_ref[...]
        st_b = st_old.astype(BF16)
        new_states = []
        y_parts = []
        for g in range(SSM_GROUPS):
            bm = xc[:, SSM_INNER + g * SSM_STATE:SSM_INNER + (g + 1) * SSM_STATE]
            cm_b = xc[:, SSM_INNER + gw + g * SSM_STATE:SSM_INNER + gw + (g + 1) * SSM_STATE].astype(BF16)
            cb = _dot_nt(cm_b, bm.astype(BF16))
            hpg = SSM_HEADS // SSM_GROUPS
            gl = slice(g * hpg * SSM_HEAD_DIM, (g + 1) * hpg * SSM_HEAD_DIM)
            y_off = _dot(cm_b, st_b[:, gl]) * ecs_e[:, gl]
            new_states.append(_dot(bm.T.astype(BF16), xdd_b[:, gl]))
            diag = []
            for pr in range(hpg // 2):
                xp = xd_b[:, gl][:, pr * LANES:(pr + 1) * LANES]
                res = []
                for hh in range(2):
                    hd = g * hpg + 2 * pr + hh
                    seg = cs[:, MISC_DT + hd:MISC_DT + hd + 1] - cs_t[MISC_DT + hd:MISC_DT + hd + 1, :]
                    decay = jnp.exp(jnp.where(tril, seg, -jnp.inf))
                    res.append(_dot((cb * decay).astype(BF16), xp))
                diag.append(jnp.where(left, res[0], res[1]))
            y_parts.append(jnp.concatenate(diag, axis=-1) + y_off)
        y = y + jnp.concatenate(y_parts, axis=-1)
        st_ref[...] = st_old * ecs_e[CHUNK - 1:CHUNK, :] + jnp.concatenate(new_states, axis=-1)

        yz = y * _silu(z_ref[0, r0:r0 + CHUNK, :].astype(F32))
        yn = yz * lax.rsqrt(jnp.mean(yz * yz, axis=-1, keepdims=True) + EPS) * gn_ref[...]
        o_ref[0, r0:r0 + CHUNK, :] = yn.astype(BF16)


def _cumsum_rows(tril_b, x):
    hi = x.astype(BF16)
    r1 = x - hi.astype(F32)
    mid = r1.astype(BF16)
    lo = (r1 - mid.astype(F32)).astype(BF16)
    return _dot(tril_b, hi) + _dot(tril_b, mid) + _dot(tril_b, lo)


def _ssd(xbc, z, misc, cw, cb, dtb, alog, dsk, gn, ts):
    bsz, s, _ = xbc.shape
    blk = lambda w: pl.BlockSpec((1, ts, w), lambda bi, j: (bi, j, 0))
    return pl.pallas_call(
        _ssd_kernel,
        grid=(bsz, s // ts),
        in_specs=[blk(SSM_XBC), blk(SSM_INNER), blk(LANES),
                  _const_spec((HALO, SSM_XBC)), _const_spec((1, SSM_XBC)),
                  _const_spec((1, LANES)), _const_spec((1, LANES)),
                  _const_spec((1, SSM_INNER)), _const_spec((1, SSM_INNER))],
        out_specs=blk(SSM_INNER),
        out_shape=jax.ShapeDtypeStruct((bsz, s, SSM_INNER), BF16),
        scratch_shapes=[pltpu.VMEM((ts + HALO, SSM_XBC), F32),
                        pltpu.VMEM((SSM_STATE, SSM_INNER), F32)],
        compiler_params=pltpu.CompilerParams(
            dimension_semantics=("arbitrary", "arbitrary"), vmem_limit_bytes=VMEM_LIMIT),
        name="ssd",
    )(xbc, z, misc, cw, cb, dtb, alog, dsk, gn)


FF_CHUNK = D_FF // 2


def _ffn_kernel(x_ref, u_ref, att_ref, y_ref, wo_ref, g_ref, wup_ref, cw_ref, cb_ref, wdn_ref,
                o_ref, ext_a, ext_g, carry_ref, gated_ref):
    tm = x_ref.shape[1]
    j = pl.program_id(1)

    @pl.when(j == 0)
    def _():
        carry_ref[...] = jnp.zeros(carry_ref.shape, F32)

    x1 = (x_ref[0]
          + _dot(u_ref[0], wo_ref[0:CONV_CH, :])
          + _dot(att_ref[0], wo_ref[CONV_CH:CONV_CH + ATT_W, :])
          + _dot(y_ref[0], wo_ref[CONV_CH + ATT_W:D_MIX, :]))
    hb = (x1 * lax.rsqrt(jnp.mean(x1 * x1, axis=-1, keepdims=True) + EPS) * g_ref[...]).astype(BF16)

    off = HALO - (FFN_CONV_K - 1)

    def conv_half(ext, c0):
        ext[0:HALO, :] = carry_ref[:, c0:c0 + FF_CHUNK]
        ext[HALO:, :] = _dot(hb, wup_ref[:, c0:c0 + FF_CHUNK])
        carry_ref[:, c0:c0 + FF_CHUNK] = ext[tm:tm + HALO, :]
        out = jnp.broadcast_to(cb_ref[:, c0:c0 + FF_CHUNK], (tm, FF_CHUNK))
        for tap in range(FFN_CONV_K):
            out = out + cw_ref[tap:tap + 1, c0:c0 + FF_CHUNK] * ext[off + tap:off + tap + tm, :]
        return out

    for c in range(D_FF // FF_CHUNK):
        fa = conv_half(ext_a, c * FF_CHUNK)
        fg = conv_half(ext_g, D_FF + c * FF_CHUNK)
        gated_ref[:, c * FF_CHUNK:(c + 1) * FF_CHUNK] = (_silu(fg) * fa).astype(BF16)
    o_ref[0] = x1 + _dot(gated_ref[...], wdn_ref[...])


def _ffn(x, u, att, y, wo, g, wup, cw, cb, wdn, tm):
    bsz, s, d = x.shape
    blk = lambda w: pl.BlockSpec((1, tm, w), lambda bi, j: (bi, j, 0))
    single = lambda shape: pl.BlockSpec(shape, lambda *_: (0,) * len(shape),
                                        pipeline_mode=pl.Buffered(1))
    return pl.pallas_call(
        _ffn_kernel,
        grid=(bsz, s // tm),
        in_specs=[blk(d), blk(CONV_CH), blk(ATT_W), blk(SSM_INNER),
                  single((D_MIX, d)), _const_spec((1, d)),
                  single((d, 2 * D_FF)), _const_spec((HALO, 2 * D_FF)), _const_spec((1, 2 * D_FF)),
                  single((D_FF, d))],
        out_specs=blk(d),
        out_shape=jax.ShapeDtypeStruct((bsz, s, d), F32),
        scratch_shapes=[pltpu.VMEM((tm + HALO, FF_CHUNK), F32),
                        pltpu.VMEM((tm + HALO, FF_CHUNK), F32),
                        pltpu.VMEM((HALO, 2 * D_FF), F32),
                        pltpu.VMEM((tm, D_FF), BF16)],
        input_output_aliases={0: 0},
        compiler_params=pltpu.CompilerParams(
            dimension_semantics=("arbitrary", "arbitrary"), vmem_limit_bytes=VMEM_LIMIT),
        name="ffn",
    )(x, u, att, y, wo, g, wup, cw, cb, wdn)


def _t5_bucket(rel):
    n = jnp.maximum(rel, 0)
    max_exact = NUM_BUCKETS // 2
    nf = jnp.maximum(n, 1).astype(F32)
    large = max_exact + (jnp.log(nf / max_exact) / math.log(MAX_DISTANCE / max_exact)
                         * (NUM_BUCKETS - max_exact)).astype(jnp.int32)
    large = jnp.minimum(large, NUM_BUCKETS - 1)
    return jnp.where(n < max_exact, n, large)


def _bias_tables(rel_bias):
    kk = jnp.arange(LANES + QB)[:, None]
    tq = jnp.arange(QB)[None, :]
    rel = tq + LANES - kk
    onehot = (_t5_bucket(rel)[..., None] == jnp.arange(NUM_BUCKETS)).astype(F32)
    near = jnp.einsum("kqb,bh->hkq", onehot, rel_bias.astype(F32),
                      precision=lax.Precision.HIGHEST)
    far = rel_bias[_t5_bucket(jnp.int32(LANES + 1))].astype(F32)
    return near, far


def _pad_rows(w, rows):
    return jnp.concatenate([w, jnp.zeros((w.shape[0], rows - w.shape[1], w.shape[2]), w.dtype)], axis=1)


def _pad_lanes(v, start, width=LANES):
    out = jnp.zeros(v.shape[:-1] + (width,), v.dtype)
    return out.at[..., start:start + v.shape[-1]].set(v)


def kernel(x, rel_bias, norm_mix_g, w_in, conv_dw_w, conv_dw_b, conv_ln_g, conv_ln_b, q_norm_g, k_norm_g, ssm_conv_w, ssm_conv_b, dt_bias, a_log, d_skip, ssm_norm_g, w_out, norm_ffn_g, w_up, ffn_conv_w, ffn_conv_b, w_down):
    bsz, s, d = x.shape
    depth = w_in.shape[0]
    assert s % KC == 0 and s >= 2 * QB

    sizes = [CONV_CH, CONV_CH, ATT_W, ATT_W, ATT_W, N_IDX_HEADS * IDX_DIM, IDX_DIM, N_IDX_HEADS,
             SSM_INNER, SSM_XBC, SSM_HEADS]
    splits = [int(v) for v in np.cumsum(sizes)[:-1]]
    ca, cg, wq, wk, wv, wiq, wik, wiw, wz, wxbc, wdt = jnp.split(w_in, splits, axis=-1)
    w_main = jnp.concatenate([ca, cg, wq, wk, wiq, wik, wik, wz, wxbc], axis=-1).astype(BF16)
    w_vt = jnp.swapaxes(wv, 1, 2).astype(BF16)
    w_misc = jnp.concatenate(
        [_pad_lanes(wiw, 0, MISC_DT), _pad_lanes(wdt, 0, LANES - MISC_DT)], axis=-1).astype(BF16)
    seg = jnp.asarray(np.kron(np.eye(N_HEADS), np.full((HEAD_DIM, HEAD_DIM), 1.0 / HEAD_DIM)), BF16)

    near_bias, far_bias = _bias_tables(rel_bias)
    logit_bound = (BOUND_SLACK * HEAD_DIM ** 0.5
                   * jnp.max(jnp.abs(q_norm_g), axis=-1, keepdims=True)
                   * jnp.max(jnp.abs(k_norm_g), axis=-1, keepdims=True)
                   + jnp.max(rel_bias, axis=0)[None, :]).astype(F32)
    head_scalars = jnp.stack([jnp.broadcast_to(far_bias, logit_bound.shape), logit_bound], axis=1)

    layer_params = dict(
        g_mix=norm_mix_g[:, None, :], w_main=w_main, w_misc=w_misc, w_vt=w_vt,
        head_scalars=head_scalars,
        gq=jnp.tile(q_norm_g, (1, N_HEADS))[:, None, :], gk=jnp.tile(k_norm_g, (1, N_HEADS))[:, None, :],
        conv_w=_pad_rows(conv_dw_w, CONV_HALO), conv_b=conv_dw_b[:, None, :],
        ln_g=conv_ln_g[:, None, :], ln_b=conv_ln_b[:, None, :],
        ssm_w=_pad_rows(ssm_conv_w, HALO), ssm_b=ssm_conv_b[:, None, :],
        dtb=_pad_lanes(dt_bias, MISC_DT)[:, None, :], alog=_pad_lanes(a_log, MISC_DT)[:, None, :],
        dsk=jnp.repeat(d_skip, SSM_HEAD_DIM, axis=-1)[:, None, :], gn=ssm_norm_g[:, None, :],
        w_out=w_out.astype(BF16), g_ffn=norm_ffn_g[:, None, :], w_up=w_up.astype(BF16),
        ffn_w=_pad_rows(ffn_conv_w, HALO), ffn_b=ffn_conv_b[:, None, :], w_down=w_down.astype(BF16),
    )

    tm = min(512, s)

    def layer(xc, p):
        (cacg, q, k, vt, iq, ik2, z, xbc, misc) = _in_proj(
            xc.reshape(bsz * s, d), p["g_mix"], p["w_main"], p["w_misc"], p["w_vt"], p["gq"], p["gk"],
            seg, tm)
        r3 = lambda a: a.reshape(bsz, s, a.shape[-1])
        u = _conv_module(r3(cacg), p["conv_w"], p["conv_b"], p["ln_g"], p["ln_b"], tm)
        att = _dsa(p["head_scalars"], r3(q), r3(k), vt.reshape(bsz, s // LANES, ATT_W, LANES), r3(iq), r3(ik2),
                   r3(misc), near_bias)
        y = _ssd(r3(xbc), r3(z), r3(misc), p["ssm_w"], p["ssm_b"], p["dtb"], p["alog"],
                 p["dsk"], p["gn"], tm)
        xn = _ffn(xc, u, att, y, p["w_out"], p["g_ffn"], p["w_up"], p["ffn_w"], p["ffn_b"],
                  p["w_down"], tm)
        return xn, None

    out, _ = lax.scan(layer, x, layer_params)
    return out
```

```python
import functools
import math

import jax
import jax.numpy as jnp
import numpy as np
from jax import lax
from jax.experimental import pallas as pl
from jax.experimental.pallas import tpu as pltpu

F32 = jnp.float32
BF16 = jnp.bfloat16

EPS = 1e-6
LANES = 128
SUBLANES = 8
CONV_CH = 512
CONV_K = 31
N_HEADS = 8
HEAD_DIM = 64
ATT_W = N_HEADS * HEAD_DIM
N_IDX_HEADS = 4
IDX_DIM = 64
TOPK_MAX = 256
NUM_BUCKETS = 32
MAX_DISTANCE = 128
SSM_HEADS = 8
SSM_HEAD_DIM = 64
SSM_INNER = SSM_HEADS * SSM_HEAD_DIM
SSM_GROUPS = 2
SSM_STATE = 128
SSM_CONV_K = 4
SSM_XBC = SSM_INNER + 2 * SSM_GROUPS * SSM_STATE
CHUNK = 128
D_MIX = CONV_CH + ATT_W + SSM_INNER
D_FF = 2816
FFN_CONV_K = 3

MISC_IW = 0
MISC_DT = 8
HALO = 8
CONV_HALO = 32
QB = 256
KC = 512
GROUPS = 256
NEG = -1e30
DENOM_FLOOR = 1e-25
BOUND_SLACK = 1.03
VMEM_LIMIT = 56 * 1024 * 1024

NT_DIMS = (((1,), (1,)), ((), ()))


def _dot(a, b):
    return jnp.dot(a, b, preferred_element_type=F32)


def _dot_nt(a, b):
    return lax.dot_general(a, b, NT_DIMS, preferred_element_type=F32)


def _dot_split(x, m_bf16):
    hi = x.astype(BF16)
    r1 = x - hi.astype(F32)
    mid = r1.astype(BF16)
    lo = (r1 - mid.astype(F32)).astype(BF16)
    return _dot(hi, m_bf16) + _dot(mid, m_bf16) + _dot(lo, m_bf16)


def _silu(x):
    return x * jax.nn.sigmoid(x)


def _const_spec(shape):
    n = len(shape)
    return pl.BlockSpec(shape, lambda *_: (0,) * n)


def _in_proj_kernel(x_ref, g_ref, w_ref, wm_ref, wvt_ref, gq_ref, gk_ref, seg_ref,
                    cacg_ref, q_ref, k_ref, vt_ref, iq_ref, ik_ref, z_ref, xbc_ref, misc_ref):
    x = x_ref[...]
    h = x * lax.rsqrt(jnp.mean(x * x, axis=-1, keepdims=True) + EPS) * g_ref[...]
    hb = h.astype(BF16)

    def proj(c0, width):
        return _dot(hb, w_ref[:, c0:c0 + width])

    cacg_ref[...] = proj(0, 1024).astype(BF16)

    def qk_norm(c0, gain_ref, scale):
        t = proj(c0, ATT_W)
        ms = _dot_split(t * t, seg_ref[...])
        return (t * lax.rsqrt(ms + EPS) * (gain_ref[...] * scale)).astype(BF16)

    q_ref[...] = qk_norm(1024, gq_ref, HEAD_DIM ** -0.5)
    k_ref[...] = qk_norm(1536, gk_ref, 1.0)
    vt = _dot_nt(wvt_ref[...], hb).astype(BF16)
    for j in range(vt_ref.shape[0]):
        vt_ref[j] = vt[:, j * LANES:(j + 1) * LANES]
    iq_ref[...] = proj(2048, N_IDX_HEADS * IDX_DIM).astype(BF16)
    ik_ref[...] = proj(2304, LANES).astype(BF16)
    z_ref[...] = proj(2432, SSM_INNER).astype(BF16)
    xbc_ref[...] = proj(2944, SSM_XBC).astype(BF16)
    misc_ref[...] = _dot(hb, wm_ref[...])


W_MAIN_COLS = 2944 + SSM_XBC


def _in_proj(x2, g, w_main, w_misc, w_vt, gq, gk, seg, tm):
    t, d = x2.shape
    rows = lambda w, dt: (pl.BlockSpec((tm, w), lambda i: (i, 0)), jax.ShapeDtypeStruct((t, w), dt))
    vt_out = (pl.BlockSpec((tm // LANES, ATT_W, LANES), lambda i: (i, 0, 0)),
              jax.ShapeDtypeStruct((t // LANES, ATT_W, LANES), BF16))
    outs = [rows(1024, BF16), rows(ATT_W, BF16), rows(ATT_W, BF16), vt_out,
            rows(N_IDX_HEADS * IDX_DIM, BF16), rows(LANES, BF16), rows(SSM_INNER, BF16),
            rows(SSM_XBC, BF16), rows(LANES, F32)]
    return pl.pallas_call(
        _in_proj_kernel,
        grid=(t // tm,),
        in_specs=[pl.BlockSpec((tm, d), lambda i: (i, 0)),
                  _const_spec((1, d)),
                  _const_spec((d, W_MAIN_COLS)),
                  _const_spec((d, LANES)),
                  _const_spec((ATT_W, d)),
                  _const_spec((1, ATT_W)), _const_spec((1, ATT_W)),
                  _const_spec((ATT_W, ATT_W))],
        out_specs=[spec for spec, _ in outs],
        out_shape=[shape for _, shape in outs],
        compiler_params=pltpu.CompilerParams(
            dimension_semantics=("arbitrary",), vmem_limit_bytes=VMEM_LIMIT),
        name="in_proj",
    )(x2, g, w_main, w_misc, w_vt, gq, gk, seg)


CONV_ROWS = 64


def _conv_kernel(cacg_ref, w_ref, b_ref, g_ref, beta_ref, o_ref, ext_ref):
    ts = o_ref.shape[1]
    j = pl.program_id(1)

    @pl.when(j == 0)
    def _():
        ext_ref[0:CONV_HALO, :] = jnp.zeros((CONV_HALO, CONV_CH), F32)

    @pl.when(j > 0)
    def _():
        ext_ref[0:CONV_HALO, :] = ext_ref[ts:ts + CONV_HALO, :]

    ca = cacg_ref[0, :, 0:CONV_CH].astype(F32)
    cg = cacg_ref[0, :, CONV_CH:2 * CONV_CH].astype(F32)
    ext_ref[CONV_HALO:, :] = ca * jax.nn.sigmoid(cg)

    w = w_ref[...]
    off = CONV_HALO - (CONV_K - 1)
    for r in range(ts // CONV_ROWS):
        base = r * CONV_ROWS
        acc = jnp.broadcast_to(b_ref[...], (CONV_ROWS, CONV_CH))
        for res in range(SUBLANES):
            rows = CONV_ROWS + (SUBLANES if res else 0)
            part = None
            for tap in range(CONV_K):
                if (off + tap) % SUBLANES != res:
                    continue
                start = base + off + tap - res
                term = w[tap:tap + 1, :] * ext_ref[start:start + rows, :]
                part = term if part is None else part + term
            acc = acc + part[res:res + CONV_ROWS, :]
        mu = jnp.mean(acc, axis=-1, keepdims=True)
        cen = acc - mu
        var = jnp.mean(cen * cen, axis=-1, keepdims=True)
        yn = cen * lax.rsqrt(var + EPS) * g_ref[...] + beta_ref[...]
        o_ref[0, base:base + CONV_ROWS, :] = _silu(yn).astype(BF16)


def _conv_module(cacg, w, b, g, beta, ts):
    bsz, s, _ = cacg.shape
    return pl.pallas_call(
        _conv_kernel,
        grid=(bsz, s // ts),
        in_specs=[pl.BlockSpec((1, ts, 2 * CONV_CH), lambda bi, j: (bi, j, 0)),
                  _const_spec((CONV_HALO, CONV_CH)),
                  _const_spec((1, CONV_CH)), _const_spec((1, CONV_CH)), _const_spec((1, CONV_CH))],
        out_specs=pl.BlockSpec((1, ts, CONV_CH), lambda bi, j: (bi, j, 0)),
        out_shape=jax.ShapeDtypeStruct((bsz, s, CONV_CH), BF16),
        scratch_shapes=[pltpu.VMEM((ts + CONV_HALO, CONV_CH), F32)],
        compiler_params=pltpu.CompilerParams(
            dimension_semantics=("arbitrary", "arbitrary"), vmem_limit_bytes=VMEM_LIMIT),
        name="conv_module",
    )(cacg, w, b, g, beta)


def _key_to_float(key):
    bits = key ^ ((key >> 31) & jnp.int32(0x7FFFFFFF))
    return lax.bitcast_convert_type(bits, F32)


def _float_to_key(v):
    bits = lax.bitcast_convert_type(v, jnp.int32)
    return bits ^ ((bits >> 31) & jnp.int32(0x7FFFFFFF))


KEY_NEG_INF = int(np.int32(np.uint32(0xFF800000)) ^ np.int32(0x7FFFFFFF))


def _dsa_kernel(hb_ref, q_ref, k_ref, vt_ref, iq_ref, ik_ref, misc_ref, nb_ref, o_ref,
                sc_ref, qm_ref, acc_ref, thr_ref, l_ref, *, topk):
    i = pl.program_id(1)
    s_len = k_ref.shape[1]
    q0 = i * QB
    lane = lax.broadcasted_iota(jnp.int32, (LANES, QB), 1)
    row = lax.broadcasted_iota(jnp.int32, (LANES, QB), 0)
    left = lax.broadcasted_iota(jnp.int32, (QB, LANES), 1) < HEAD_DIM
    qpos = q0 + lane

    iq = iq_ref[0]
    misc_t = misc_ref[0].T
    iqm, iw = [], []
    for h in range(N_IDX_HEADS):
        pair = iq[:, (h // 2) * LANES:(h // 2 + 1) * LANES]
        iqm.append(jnp.where(left if h % 2 == 0 else ~left, pair, jnp.zeros_like(pair)))
        iw.append(misc_t[MISC_IW + h:MISC_IW + h + 1, :] * (IDX_DIM ** -0.5 * N_IDX_HEADS ** -0.5))

    def fold(x, op):
        return op(x.reshape(LANES // SUBLANES, SUBLANES, QB), axis=0)

    def score_chunk(c, carry):
        gmax, n_ge0, n_gt0 = carry
        gmax = list(gmax)
        k0 = pl.multiple_of(c * KC, KC)
        ikc = ik_ref[0, pl.ds(k0, KC), :]
        s = jnp.zeros((KC, QB), F32)
        for h in range(N_IDX_HEADS):
            s = s + jnp.maximum(_dot_nt(ikc, iqm[h]), 0.0) * iw[h]
        for t in range(KC // LANES):
            kpos = k0 + t * LANES + row
            blk = jnp.where(kpos <= qpos, s[t * LANES:(t + 1) * LANES, :], -jnp.inf)
            sc_ref[c * (KC // LANES) + t] = blk
            g = t % (GROUPS // LANES)
            gmax[g] = jnp.maximum(gmax[g], blk)
            n_ge0 = n_ge0 + fold(jnp.where(blk >= 0.0, 1.0, 0.0), jnp.sum)
            n_gt0 = n_gt0 + fold(jnp.where(blk > 0.0, 1.0, 0.0), jnp.sum)
        return tuple(gmax), n_ge0, n_gt0

    per_chunk = KC // LANES
    n_chunks = (q0 + QB - 1) // KC + 1
    gmax, n_ge0, n_gt0 = lax.fori_loop(
        0, n_chunks, score_chunk,
        (tuple(jnp.full((LANES, QB), -jnp.inf, F32) for _ in range(GROUPS // LANES)),
         jnp.zeros((SUBLANES, QB), F32), jnp.zeros((SUBLANES, QB), F32)))

    def count_blocks(pred):
        def body(c, acc):
            for t in range(per_chunk):
                bk = c * per_chunk + t
                acc = acc + jnp.where(pred(sc_ref[bk], bk), 1.0, 0.0)
            return acc
        acc = lax.fori_loop(0, n_chunks, body, jnp.zeros((LANES, QB), F32))
        return jnp.sum(acc, axis=0, keepdims=True)

    everything = jnp.float32(2 * s_len)
    vmax = jnp.max(functools.reduce(jnp.maximum, gmax), axis=0, keepdims=True)
    glow = jnp.min(functools.reduce(jnp.minimum, gmax), axis=0, keepdims=True)
    at0 = jnp.sum(n_ge0, axis=0, keepdims=True)
    above0 = jnp.sum(n_gt0, axis=0, keepdims=True)
    key_glow = _float_to_key(glow)
    positive = above0 >= topk
    negative = at0 < topk
    lo0 = jnp.where(positive, jnp.maximum(key_glow, 1), key_glow)
    c_lo0 = jnp.where(positive & (key_glow < 1), above0, everything)
    hi0 = jnp.where(negative, 0, _float_to_key(vmax) + 1)
    c_hi0 = jnp.where(negative, at0, 0.0)
    zero_tied = jnp.logical_not(positive | negative)
    closed0 = lo0 + 1 >= hi0
    done0 = zero_tied | closed0
    thr0 = jnp.where(zero_tied, 0, lo0)
    n_ge_init = jnp.where(zero_tied, at0, c_lo0)
    need0 = jnp.where(zero_tied, topk - above0, topk - c_hi0)

    steps_per_check = 4

    def search_cond(state):
        it, _, _, _, _, done, _, _, _ = state
        return (it < 36) & (jnp.min(done) == 0)

    def search_steps(state):
        return lax.fori_loop(0, steps_per_check, lambda _, st: search_step(st), state)

    def search_step(state):
        it, lo, hi, c_lo, c_hi, done, thr_key, n_ge, need = state
        mid = (lo >> 1) + (hi >> 1) + (lo & hi & 1)
        mid_b = jnp.broadcast_to(_key_to_float(mid), (LANES, QB))
        cnt = count_blocks(lambda s, bk: s >= mid_b)
        ok = cnt >= topk
        lo2, c_lo2 = jnp.where(ok, mid, lo), jnp.where(ok, cnt, c_lo)
        hi2, c_hi2 = jnp.where(ok, hi, mid), jnp.where(ok, c_hi, cnt)
        hit = cnt == topk
        newly = (done == 0) & (hit | (lo2 + 1 == hi2))
        thr_key = jnp.where(newly, jnp.where(hit, mid, lo2), thr_key)
        n_ge = jnp.where(newly, jnp.where(hit, float(topk), c_lo2), n_ge)
        need = jnp.where(newly, jnp.where(hit, everything, topk - c_hi2), need)
        frozen = done == 1
        return (it + 1, jnp.where(frozen, lo, lo2), jnp.where(frozen, hi, hi2),
                jnp.where(frozen, c_lo, c_lo2), jnp.where(frozen, c_hi, c_hi2),
                jnp.where(newly, 1, done), thr_key, n_ge, need)

    state = lax.while_loop(
        search_cond, search_steps,
        (jnp.int32(0), lo0, hi0, c_lo0, c_hi0, done0.astype(jnp.int32), thr0, n_ge_init, need0))
    thr = _key_to_float(state[6])
    n_ge, need = state[7], state[8]
    thr_b = jnp.broadcast_to(thr, (LANES, QB))
    thr_ref[...] = thr_b

    @pl.when(jnp.max(n_ge) > topk)
    def _():
        rank_mat = jnp.where(
            lax.broadcasted_iota(jnp.int32, (LANES, LANES), 0)
            >= lax.broadcasted_iota(jnp.int32, (LANES, LANES), 1), 1.0, 0.0).astype(BF16)

        def demote(c, seen):
            for t in range(per_chunk):
                bk = c * per_chunk + t
                s = sc_ref[bk]
                tie = s == thr_b
                ind = jnp.where(tie, 1.0, 0.0)
                rank = seen + _dot(rank_mat, ind.astype(BF16))
                sc_ref[bk] = jnp.where(tie & (rank > need), -jnp.inf, s)
                seen = seen + jnp.sum(ind, axis=0, keepdims=True)
            return seen

        lax.fori_loop(0, n_chunks, demote, jnp.zeros((1, QB), F32))

    q = q_ref[0]
    for pr in range(N_HEADS // 2):
        pair = q[:, pr * LANES:(pr + 1) * LANES]
        zero = jnp.zeros_like(pair)
        qm_ref[pr] = jnp.concatenate([jnp.where(left, pair, zero), jnp.where(left, zero, pair)], axis=0)
    odd =lax.broadcasted_iota(jnp.int32, (1, 2 * QB), 1) >= QB

    def mask_tile(bk, active, causal):
        t = jnp.where(active, thr_ref[...], jnp.inf)
        sel = sc_ref[bk] >= t
        if causal:
            sel = sel & (bk * LANES + row <= qpos)
        return jnp.where(sel, 0.0, NEG)

    def pair_row(values, pr):
        return jnp.where(odd, values[2 * pr + 1], values[2 * pr])

    def attend(blocks, maskadd, near, state, robust):
        width = len(blocks) * LANES
        k0 = pl.multiple_of(blocks[0] * LANES, LANES)
        maskadd2 = jnp.concatenate([maskadd, maskadd], axis=1)
        far = [hb_ref[0, h] for h in range(N_HEADS)]
        bound = [hb_ref[1, h] for h in range(N_HEADS)]
        new_state = []
        for pr in range(N_HEADS // 2):
            kp = k_ref[0, pl.ds(k0, width), pr * LANES:(pr + 1) * LANES]
            lg = _dot_nt(kp, qm_ref[pr]) + maskadd2
            if near is None:
                bias = pair_row(far, pr)
            else:
                lg = lg + jnp.concatenate(
                    [nb_ref[2 * pr + hh, near:near + width, :] for hh in range(2)], axis=1)
                bias = 0.0
            if robust:
                m_old, l_old = state[pr]
                m_new = jnp.maximum(m_old, jnp.max(lg, axis=0, keepdims=True) + bias)
                alpha = jnp.exp(m_old - m_new)
                p = jnp.exp(lg - (m_new - bias))
                new_state.append((m_new, alpha * l_old + jnp.sum(p, axis=0, keepdims=True)))
            else:
                alpha = None
                p = jnp.exp(lg + (bias - pair_row(bound, pr)))
                new_state.append(state[pr] + jnp.sum(p, axis=0, keepdims=True))
            rows = slice(pr * LANES, (pr + 1) * LANES)
            vt = jnp.concatenate([vt_ref[0, bk, rows, :] for bk in blocks], axis=1)
            pv = _dot(vt, p.astype(BF16))
            for hh in range(2):
                hs = slice(pr * LANES + hh * HEAD_DIM, pr * LANES + (hh + 1) * HEAD_DIM)
                cs = slice(hh * QB, (hh + 1) * QB)
                upd = pv[hh * HEAD_DIM:(hh + 1) * HEAD_DIM, cs]
                acc_ref[hs, :] = acc_ref[hs, :] + upd if alpha is None else alpha[:, cs] * acc_ref[hs, :] + upd
        return new_state

    far_end = jnp.maximum(q0 - LANES, 0)
    diag = [q0 // LANES + t for t in range(QB // LANES)]
    prev = jnp.maximum(q0 // LANES - 1, 0)

    def attention(robust):
        acc_ref[...] = jnp.zeros(acc_ref.shape, F32)
        zero = jnp.zeros((1, 2 * QB), F32)
        state = [(jnp.full((1, 2 * QB), NEG, F32), zero) if robust else zero
                 for _ in range(N_HEADS // 2)]

        def far_chunk(c, state):
            blocks = [c * per_chunk + t for t in range(per_chunk)]
            maskadd = jnp.concatenate(
                [mask_tile(bk, bk * LANES < far_end, False) for bk in blocks], axis=0)
            return attend(blocks, maskadd, None, state, robust)

        state = lax.fori_loop(0, (far_end + KC - 1) // KC, far_chunk, state)
        state = attend([prev], mask_tile(prev, i >= 1, False), 0, state, robust)
        state = attend(diag, jnp.concatenate([mask_tile(bk, True, True) for bk in diag], axis=0),
                       LANES, state, robust)
        for pr in range(N_HEADS // 2):
            l_ref[pr:pr + 1, :] = state[pr][1] if robust else state[pr]

    attention(robust=False)

    @pl.when(jnp.logical_not(jnp.min(l_ref[...]) > DENOM_FLOOR))
    def _():
        attention(robust=True)

    out_t = jnp.concatenate(
        [acc_ref[h * HEAD_DIM:(h + 1) * HEAD_DIM, :]
         / l_ref[h // 2:h // 2 + 1, (h % 2) * QB:(h % 2 + 1) * QB] for h in range(N_HEADS)],
        axis=0)
    o_ref[0] = out_t.T.astype(BF16)


def _dsa(head_scalars, q, k, vt, iq, ik2, misc, near_bias):
    bsz, s, _ = q.shape
    topk = min(TOPK_MAX, s // 4)
    nq = s // QB
    blk = lambda w: pl.BlockSpec((1, QB, w), lambda bi, i: (bi, i, 0))
    once = pl.Buffered(1)
    full = lambda w: pl.BlockSpec((1, s, w), lambda bi, i: (bi, 0, 0), pipeline_mode=once)
    return pl.pallas_call(
        functools.partial(_dsa_kernel, topk=topk),
        grid=(bsz, nq),
        in_specs=[pl.BlockSpec(memory_space=pltpu.SMEM),
                  blk(ATT_W), full(ATT_W),
                  pl.BlockSpec((1, s // LANES, ATT_W, LANES), lambda bi, i: (bi, 0, 0, 0),
                               pipeline_mode=once),
                  blk(N_IDX_HEADS * IDX_DIM), full(LANES), blk(LANES),
                  pl.BlockSpec((N_HEADS, LANES + QB, QB), lambda bi, i: (0, 0, 0), pipeline_mode=once)],
        out_specs=blk(ATT_W),
        out_shape=jax.ShapeDtypeStruct((bsz, s, ATT_W), BF16),
        scratch_shapes=[
            pltpu.VMEM((_round_up(s, KC) // LANES, LANES, QB), F32),
            pltpu.VMEM((N_HEADS // 2, 2 * QB, LANES), BF16),
            pltpu.VMEM((ATT_W, QB), F32),
            pltpu.VMEM((LANES, QB), F32),
            pltpu.VMEM((N_HEADS // 2, 2 * QB), F32),
        ],
        compiler_params=pltpu.CompilerParams(
            dimension_semantics=("arbitrary", "arbitrary"), vmem_limit_bytes=VMEM_LIMIT),
        name="dsa",
    )(head_scalars, q, k, vt, iq, ik2, misc, near_bias)


def _round_up(a, b):
    return (a + b - 1) // b * b


def _ssd_kernel(xbc_ref, z_ref, misc_ref, cw_ref, cb_ref, dtb_ref, alog_ref, dsk_ref, gn_ref,
                o_ref, ext_ref, st_ref):
    ts = o_ref.shape[1]
    j = pl.program_id(1)

    @pl.when(j == 0)
    def _():
        ext_ref[0:HALO, :] = jnp.zeros((HALO, SSM_XBC), F32)
        st_ref[...] = jnp.zeros(st_ref.shape, F32)

    @pl.when(j > 0)
    def _():
        ext_ref[0:HALO, :] = ext_ref[ts:ts + HALO, :]

    ext_ref[HALO:, :] = xbc_ref[0].astype(F32)

    lane = lax.broadcasted_iota(jnp.int32, (CHUNK, LANES), 1)
    row = lax.broadcasted_iota(jnp.int32, (CHUNK, LANES), 0)
    tril = row >= lane
    tril_b = jnp.where(tril, 1.0, 0.0).astype(BF16)
    is_dt = (lane >= MISC_DT) & (lane < MISC_DT + SSM_HEADS)
    e_row = lax.broadcasted_iota(jnp.int32, (LANES, SSM_INNER), 0)
    e_col = lax.broadcasted_iota(jnp.int32, (LANES, SSM_INNER), 1)
    expand = jnp.where(e_row - MISC_DT == e_col // SSM_HEAD_DIM, 1.0, 0.0).astype(BF16)
    left = lane < SSM_HEAD_DIM
    lane1 = lax.broadcasted_iota(jnp.int32, (1, LANES), 1)
    a_row = jnp.where((lane1 >= MISC_DT) & (lane1 < MISC_DT + SSM_HEADS), -jnp.exp(alog_ref[...]), 0.0)
    cw = cw_ref[...]
    off = HALO - (SSM_CONV_K - 1)
    gw = SSM_GROUPS * SSM_STATE

    for c in range(ts // CHUNK):
        r0 = c * CHUNK
        conv = jnp.broadcast_to(cb_ref[...], (CHUNK, SSM_XBC))
        for tap in range(SSM_CONV_K):
            conv = conv + cw[tap:tap + 1, :] * ext_ref[r0 + off + tap:r0 + off + tap + CHUNK, :]
        xc = _silu(conv)
        xs = xc[:, 0:SSM_INNER]

        dt = jnp.where(is_dt, jax.nn.softplus(misc_ref[0, r0:r0 + CHUNK, :] + dtb_ref[...]), 0.0)
        cs = _cumsum_rows(tril_b, dt * a_row)
        cs_last = cs[CHUNK - 1:CHUNK, :]
        dt_e = _dot_split(dt, expand)
        dec_e = _dot_split(jnp.exp(cs_last - cs), expand)
        ecs_e = _dot_split(jnp.exp(cs), expand)
        cs_t = cs.T

        xd = xs * dt_e
        xd_b = xd.astype(BF16)
        xdd_b = (xd * dec_e).astype(BF16)
        y = dsk_ref[...] * xs
        st_old = st_ref[...]
        st_b = st_old.astype(BF16)
        new_states = []
        y_parts = []
        for g in range(SSM_GROUPS):
            bm = xc[:, SSM_INNER + g * SSM_STATE:SSM_INNER + (g + 1) * SSM_STATE]
            cm_b = xc[:, SSM_INNER + gw + g * SSM_STATE:SSM_INNER + gw + (g + 1) * SSM_STATE].astype(BF16)
            cb = _dot_nt(cm_b, bm.astype(BF16))
            hpg = SSM_HEADS // SSM_GROUPS
            gl = slice(g * hpg * SSM_HEAD_DIM, (g + 1) * hpg * SSM_HEAD_DIM)
            y_off = _dot(cm_b, st_b[:, gl]) * ecs_e[:, gl]
            new_states.append(_dot(bm.T.astype(BF16), xdd_b[:, gl]))
            diag = []
            for pr in range(hpg // 2):
                xp = xd_b[:, gl][:, pr * LANES:(pr + 1) * LANES]
                res = []
                for hh in range(2):
                    hd = g * hpg + 2 * pr + hh
                    seg = cs[:, MISC_DT + hd:MISC_DT + hd + 1] - cs_t[MISC_DT + hd:MISC_DT + hd + 1, :]
                    decay = jnp.exp(jnp.where(tril, seg, -jnp.inf))
                    res.append(_dot((cb * decay).astype(BF16), xp))
                diag.append(jnp.where(left, res[0], res[1]))
            y_parts.append(jnp.concatenate(diag, axis=-1) + y_off)
        y = y + jnp.concatenate(y_parts, axis=-1)
        st_ref[...] = st_old * ecs_e[CHUNK - 1:CHUNK, :] + jnp.concatenate(new_states, axis=-1)

        yz = y * _silu(z_ref[0, r0:r0 + CHUNK, :].astype(F32))
        yn = yz * lax.rsqrt(jnp.mean(yz * yz, axis=-1, keepdims=True) + EPS) * gn_ref[...]
        o_ref[0, r0:r0 + CHUNK, :] = yn.astype(BF16)


def _cumsum_rows(tril_b, x):
    hi = x.astype(BF16)
    r1 = x - hi.astype(F32)
    mid = r1.astype(BF16)
    lo = (r1 - mid.astype(F32)).astype(BF16)
    return _dot(tril_b, hi) + _dot(tril_b, mid) + _dot(tril_b, lo)


def _ssd(xbc, z, misc, cw, cb, dtb, alog, dsk, gn, ts):
    bsz, s, _ = xbc.shape
    blk = lambda w: pl.BlockSpec((1, ts, w), lambda bi, j: (bi, j, 0))
    return pl.pallas_call(
        _ssd_kernel,
        grid=(bsz, s // ts),
        in_specs=[blk(SSM_XBC), blk(SSM_INNER), blk(LANES),
                  _const_spec((HALO, SSM_XBC)), _const_spec((1, SSM_XBC)),
                  _const_spec((1, LANES)), _const_spec((1, LANES)),
                  _const_spec((1, SSM_INNER)), _const_spec((1, SSM_INNER))],
        out_specs=blk(SSM_INNER),
        out_shape=jax.ShapeDtypeStruct((bsz, s, SSM_INNER), BF16),
        scratch_shapes=[pltpu.VMEM((ts + HALO, SSM_XBC), F32),
                        pltpu.VMEM((SSM_STATE, SSM_INNER), F32)],
        compiler_params=pltpu.CompilerParams(
            dimension_semantics=("arbitrary", "arbitrary"), vmem_limit_bytes=VMEM_LIMIT),
        name="ssd",
    )(xbc, z, misc, cw, cb, dtb, alog, dsk, gn)


FF_CHUNK = D_FF // 2


def _ffn_kernel(x_ref, u_ref, att_ref, y_ref, wo_ref, g_ref, wup_ref, cw_ref, cb_ref, wdn_ref,
                o_ref, ext_a, ext_g, carry_ref, gated_ref):
    tm = x_ref.shape[1]
    j = pl.program_id(1)

    @pl.when(j == 0)
    def _():
        carry_ref[...] = jnp.zeros(carry_ref.shape, F32)

    x1 = (x_ref[0]
          + _dot(u_ref[0], wo_ref[0:CONV_CH, :])
          + _dot(att_ref[0], wo_ref[CONV_CH:CONV_CH + ATT_W, :])
          + _dot(y_ref[0], wo_ref[CONV_CH + ATT_W:D_MIX, :]))
    hb = (x1 * lax.rsqrt(jnp.mean(x1 * x1, axis=-1, keepdims=True) + EPS) * g_ref[...]).astype(BF16)

    off = HALO - (FFN_CONV_K - 1)

    def conv_half(ext, c0):
        ext[0:HALO, :] = carry_ref[:, c0:c0 + FF_CHUNK]
        ext[HALO:, :] = _dot(hb, wup_ref[:, c0:c0 + FF_CHUNK])
        carry_ref[:, c0:c0 + FF_CHUNK] = ext[tm:tm + HALO, :]
        out = jnp.broadcast_to(cb_ref[:, c0:c0 + FF_CHUNK], (tm, FF_CHUNK))
        for tap in range(FFN_CONV_K):
            out = out + cw_ref[tap:tap + 1, c0:c0 + FF_CHUNK] * ext[off + tap:off + tap + tm, :]
        return out

    for c in range(D_FF // FF_CHUNK):
        fa = conv_half(ext_a, c * FF_CHUNK)
        fg = conv_half(ext_g, D_FF + c * FF_CHUNK)
        gated_ref[:, c * FF_CHUNK:(c + 1) * FF_CHUNK] = (_silu(fg) * fa).astype(BF16)
    o_ref[0] = x1 + _dot(gated_ref[...], wdn_ref[...])


def _ffn(x, u, att, y, wo, g, wup, cw, cb, wdn, tm):
    bsz, s, d = x.shape
    blk = lambda w: pl.BlockSpec((1, tm, w), lambda bi, j: (bi, j, 0))
    single = lambda shape: pl.BlockSpec(shape, lambda *_: (0,) * len(shape),
                                        pipeline_mode=pl.Buffered(1))
    return pl.pallas_call(
        _ffn_kernel,
        grid=(bsz, s // tm),
        in_specs=[blk(d), blk(CONV_CH), blk(ATT_W), blk(SSM_INNER),
                  single((D_MIX, d)), _const_spec((1, d)),
                  single((d, 2 * D_FF)), _const_spec((HALO, 2 * D_FF)), _const_spec((1, 2 * D_FF)),
                  single((D_FF, d))],
        out_specs=blk(d),
        out_shape=jax.ShapeDtypeStruct((bsz, s, d), F32),
        scratch_shapes=[pltpu.VMEM((tm + HALO, FF_CHUNK), F32),
                        pltpu.VMEM((tm + HALO, FF_CHUNK), F32),
                        pltpu.VMEM((HALO, 2 * D_FF), F32),
                        pltpu.VMEM((tm, D_FF), BF16)],
        input_output_aliases={0: 0},
        compiler_params=pltpu.CompilerParams(
            dimension_semantics=("arbitrary", "arbitrary"), vmem_limit_bytes=VMEM_LIMIT),
        name="ffn",
    )(x, u, att, y, wo, g, wup, cw, cb, wdn)


def _t5_bucket(rel):
    n = jnp.maximum(rel, 0)
    max_exact = NUM_BUCKETS // 2
    nf = jnp.maximum(n, 1).astype(F32)
    large = max_exact + (jnp.log(nf / max_exact) / math.log(MAX_DISTANCE / max_exact)
                         * (NUM_BUCKETS - max_exact)).astype(jnp.int32)
    large = jnp.minimum(large, NUM_BUCKETS - 1)
    return jnp.where(n < max_exact, n, large)


def _bias_tables(rel_bias):
    kk = jnp.arange(LANES + QB)[:, None]
    tq = jnp.arange(QB)[None, :]
    rel = tq + LANES - kk
    onehot = (_t5_bucket(rel)[..., None] == jnp.arange(NUM_BUCKETS)).astype(F32)
    near = jnp.einsum("kqb,bh->hkq", onehot, rel_bias.astype(F32),
                      precision=lax.Precision.HIGHEST)
    far = rel_bias[_t5_bucket(jnp.int32(LANES + 1))].astype(F32)
    return near, far


def _pad_rows(w, rows):
    return jnp.concatenate([w, jnp.zeros((w.shape[0], rows - w.shape[1], w.shape[2]), w.dtype)], axis=1)


def _pad_lanes(v, start, width=LANES):
    out = jnp.zeros(v.shape[:-1] + (width,), v.dtype)
    return out.at[..., start:start + v.shape[-1]].set(v)


def kernel(x, rel_bias, norm_mix_g, w_in, conv_dw_w, conv_dw_b, conv_ln_g, conv_ln_b, q_norm_g, k_norm_g, ssm_conv_w, ssm_conv_b, dt_bias, a_log, d_skip, ssm_norm_g, w_out, norm_ffn_g, w_up, ffn_conv_w, ffn_conv_b, w_down):
    bsz, s, d = x.shape
    depth = w_in.shape[0]
    assert s % KC == 0 and s >= 2 * QB

    sizes = [CONV_CH, CONV_CH, ATT_W, ATT_W, ATT_W, N_IDX_HEADS * IDX_DIM, IDX_DIM, N_IDX_HEADS,
             SSM_INNER, SSM_XBC, SSM_HEADS]
    splits = [int(v) for v in np.cumsum(sizes)[:-1]]
    ca, cg, wq, wk, wv, wiq, wik, wiw, wz, wxbc, wdt = jnp.split(w_in, splits, axis=-1)
    w_main = jnp.concatenate([ca, cg, wq, wk, wiq, wik, wik, wz, wxbc], axis=-1).astype(BF16)
    w_vt = jnp.swapaxes(wv, 1, 2).astype(BF16)
    w_misc = jnp.concatenate(
        [_pad_lanes(wiw, 0, MISC_DT), _pad_lanes(wdt, 0, LANES - MISC_DT)], axis=-1).astype(BF16)
    seg = jnp.asarray(np.kron(np.eye(N_HEADS), np.full((HEAD_DIM, HEAD_DIM), 1.0 / HEAD_DIM)), BF16)

    near_bias, far_bias = _bias_tables(rel_bias)
    logit_bound = (BOUND_SLACK * HEAD_DIM ** 0.5
                   * jnp.max(jnp.abs(q_norm_g), axis=-1, keepdims=True)
                   * jnp.max(jnp.abs(k_norm_g), axis=-1, keepdims=True)
                   + jnp.max(rel_bias, axis=0)[None, :]).astype(F32)
    head_scalars = jnp.stack([jnp.broadcast_to(far_bias, logit_bound.shape), logit_bound], axis=1)

    layer_params = dict(
        g_mix=norm_mix_g[:, None, :], w_main=w_main, w_misc=w_misc, w_vt=w_vt,
        head_scalars=head_scalars,
        gq=jnp.tile(q_norm_g, (1, N_HEADS))[:, None, :], gk=jnp.tile(k_norm_g, (1, N_HEADS))[:, None, :],
        conv_w=_pad_rows(conv_dw_w, CONV_HALO), conv_b=conv_dw_b[:, None, :],
        ln_g=conv_ln_g[:, None, :], ln_b=conv_ln_b[:, None, :],
        ssm_w=_pad_rows(ssm_conv_w, HALO), ssm_b=ssm_conv_b[:, None, :],
        dtb=_pad_lanes(dt_bias, MISC_DT)[:, None, :], alog=_pad_lanes(a_log, MISC_DT)[:, None, :],
        dsk=jnp.repeat(d_skip, SSM_HEAD_DIM, axis=-1)[:, None, :], gn=ssm_norm_g[:, None, :],
        w_out=w_out.astype(BF16), g_ffn=norm_ffn_g[:, None, :], w_up=w_up.astype(BF16),
        ffn_w=_pad_rows(ffn_conv_w, HALO), ffn_b=ffn_conv_b[:, None, :], w_down=w_down.astype(BF16),
    )

    tm = min(512, s)

    def layer(xc, p):
        (cacg, q, k, vt, iq, ik2, z, xbc, misc) = _in_proj(
            xc.reshape(bsz * s, d), p["g_mix"], p["w_main"], p["w_misc"], p["w_vt"], p["gq"], p["gk"],
            seg, tm)
        r3 = lambda a: a.reshape(bsz, s, a.shape[-1])
        u = _conv_module(r3(cacg), p["conv_w"], p["conv_b"], p["ln_g"], p["ln_b"], tm)
        att = _dsa(p["head_scalars"], r3(q), r3(k), vt.reshape(bsz, s // LANES, ATT_W, LANES), r3(iq), r3(ik2),
                   r3(misc), near_bias)
        y = _ssd(r3(xbc), r3(z), r3(misc), p["ssm_w"], p["ssm_b"], p["dtb"], p["alog"],
                 p["dsk"], p["gn"], tm)
        xn = _ffn(xc, u, att, y, p["w_out"], p["g_ffn"], p["w_up"], p["ffn_w"], p["ffn_b"],
                  p["w_down"], tm)
        return xn, None

    out, _ = lax.scan(layer, x, layer_params)
    return out
```

```python
import functools
import math

import jax
import jax.numpy as jnp
import numpy as np
from jax import lax
from jax.experimental import pallas as pl
from jax.experimental.pallas import tpu as pltpu

F32 = jnp.float32
BF16 = jnp.bfloat16

EPS = 1e-6
LANES = 128
SUBLANES = 8
CONV_CH = 512
CONV_K = 31
N_HEADS = 8
HEAD_DIM = 64
ATT_W = N_HEADS * HEAD_DIM
N_IDX_HEADS = 4
IDX_DIM = 64
TOPK_MAX = 256
NUM_BUCKETS = 32
MAX_DISTANCE = 128
SSM_HEADS = 8
SSM_HEAD_DIM = 64
SSM_INNER = SSM_HEADS * SSM_HEAD_DIM
SSM_GROUPS = 2
SSM_STATE = 128
SSM_CONV_K = 4
SSM_XBC = SSM_INNER + 2 * SSM_GROUPS * SSM_STATE
CHUNK = 128
D_MIX = CONV_CH + ATT_W + SSM_INNER
D_FF = 2816
FFN_CONV_K = 3

MISC_IW = 0
MISC_DT = 8
HALO = 8
CONV_HALO = 32
QB = 256
KC = 512
GROUPS = 256
NEG = -1e30
DENOM_FLOOR = 1e-25
BOUND_SLACK = 1.03
VMEM_LIMIT = 56 * 1024 * 1024

NT_DIMS = (((1,), (1,)), ((), ()))


def _dot(a, b):
    return jnp.dot(a, b, preferred_element_type=F32)


def _dot_nt(a, b):
    return lax.dot_general(a, b, NT_DIMS, preferred_element_type=F32)


def _dot_split(x, m_bf16):
    hi = x.astype(BF16)
    r1 = x - hi.astype(F32)
    mid = r1.astype(BF16)
    lo = (r1 - mid.astype(F32)).astype(BF16)
    return _dot(hi, m_bf16) + _dot(mid, m_bf16) + _dot(lo, m_bf16)


def _silu(x):
    return x * jax.nn.sigmoid(x)


def _const_spec(shape):
    n = len(shape)
    return pl.BlockSpec(shape, lambda *_: (0,) * n)


def _in_proj_kernel(x_ref, g_ref, w_ref, wm_ref, wvt_ref, gq_ref, gk_ref, seg_ref,
                    cacg_ref, q_ref, k_ref, vt_ref, iq_ref, ik_ref, z_ref, xbc_ref, misc_ref):
    x = x_ref[...]
    h = x * lax.rsqrt(jnp.mean(x * x, axis=-1, keepdims=True) + EPS) * g_ref[...]
    hb = h.astype(BF16)

    def proj(c0, width):
        return _dot(hb, w_ref[:, c0:c0 + width])

    cacg_ref[...] = proj(0, 1024).astype(BF16)

    def qk_norm(c0, gain_ref, scale):
        t = proj(c0, ATT_W)
        ms = _dot((t * t).astype(BF16), seg_ref[...])
        return (t * lax.rsqrt(ms + EPS) * (gain_ref[...] * scale)).astype(BF16)

    q_ref[...] = qk_norm(1024, gq_ref, HEAD_DIM ** -0.5)
    k_ref[...] = qk_norm(1536, gk_ref, 1.0)
    vt = _dot_nt(wvt_ref[...], hb).astype(BF16)
    for j in range(vt_ref.shape[0]):
        vt_ref[j] = vt[:, j * LANES:(j + 1) * LANES]
    iq_ref[...] = proj(2048, N_IDX_HEADS * IDX_DIM).astype(BF16)
    ik_ref[...] = proj(2304, LANES).astype(BF16)
    z_ref[...] = proj(2432, SSM_INNER).astype(BF16)
    xbc_ref[...] = proj(2944, SSM_XBC).astype(BF16)
    misc_ref[...] = _dot(hb, wm_ref[...])


W_MAIN_COLS = 2944 + SSM_XBC


def _in_proj(x2, g, w_main, w_misc, w_vt, gq, gk, seg, tm):
    t, d = x2.shape
    rows = lambda w, dt: (pl.BlockSpec((tm, w), lambda i: (i, 0)), jax.ShapeDtypeStruct((t, w), dt))
    vt_out = (pl.BlockSpec((tm // LANES, ATT_W, LANES), lambda i: (i, 0, 0)),
              jax.ShapeDtypeStruct((t // LANES, ATT_W, LANES), BF16))
    outs = [rows(1024, BF16), rows(ATT_W, BF16), rows(ATT_W, BF16), vt_out,
            rows(N_IDX_HEADS * IDX_DIM, BF16), rows(LANES, BF16), rows(SSM_INNER, BF16),
            rows(SSM_XBC, BF16), rows(LANES, F32)]
    return pl.pallas_call(
        _in_proj_kernel,
        grid=(t // tm,),
        in_specs=[pl.BlockSpec((tm, d), lambda i: (i, 0)),
                  _const_spec((1, d)),
                  _const_spec((d, W_MAIN_COLS)),
                  _const_spec((d, LANES)),
                  _const_spec((ATT_W, d)),
                  _const_spec((1, ATT_W)), _const_spec((1, ATT_W)),
                  _const_spec((ATT_W, ATT_W))],
        out_specs=[spec for spec, _ in outs],
        out_shape=[shape for _, shape in outs],
        compiler_params=pltpu.CompilerParams(
            dimension_semantics=("arbitrary",), vmem_limit_bytes=VMEM_LIMIT),
        name="in_proj",
    )(x2, g, w_main, w_misc, w_vt, gq, gk, seg)


CONV_ROWS = 64


def _conv_kernel(cacg_ref, w_ref, b_ref, g_ref, beta_ref, o_ref, ext_ref):
    ts = o_ref.shape[1]
    j = pl.program_id(1)

    @pl.when(j == 0)
    def _():
        ext_ref[0:CONV_HALO, :] = jnp.zeros((CONV_HALO, CONV_CH), F32)

    @pl.when(j > 0)
    def _():
        ext_ref[0:CONV_HALO, :] = ext_ref[ts:ts + CONV_HALO, :]

    ca = cacg_ref[0, :, 0:CONV_CH].astype(F32)
    cg = cacg_ref[0, :, CONV_CH:2 * CONV_CH].astype(F32)
    ext_ref[CONV_HALO:, :] = ca * jax.nn.sigmoid(cg)

    w = w_ref[...]
    off = CONV_HALO - (CONV_K - 1)
    for r in range(ts // CONV_ROWS):
        base = r * CONV_ROWS
        acc = jnp.broadcast_to(b_ref[...], (CONV_ROWS, CONV_CH))
        for res in range(SUBLANES):
            rows = CONV_ROWS + (SUBLANES if res else 0)
            part = None
            for tap in range(CONV_K):
                if (off + tap) % SUBLANES != res:
                    continue
                start = base + off + tap - res
                term = w[tap:tap + 1, :] * ext_ref[start:start + rows, :]
                part = term if part is None else part + term
            acc = acc + part[res:res + CONV_ROWS, :]
        mu = jnp.mean(acc, axis=-1, keepdims=True)
        cen = acc - mu
        var = jnp.mean(cen * cen, axis=-1, keepdims=True)
        yn = cen * lax.rsqrt(var + EPS) * g_ref[...] + beta_ref[...]
        o_ref[0, base:base + CONV_ROWS, :] = _silu(yn).astype(BF16)


def _conv_module(cacg, w, b, g, beta, ts):
    bsz, s, _ = cacg.shape
    return pl.pallas_call(
        _conv_kernel,
        grid=(bsz, s // ts),
        in_specs=[pl.BlockSpec((1, ts, 2 * CONV_CH), lambda bi, j: (bi, j, 0)),
                  _const_spec((CONV_HALO, CONV_CH)),
                  _const_spec((1, CONV_CH)), _const_spec((1, CONV_CH)), _const_spec((1, CONV_CH))],
        out_specs=pl.BlockSpec((1, ts, CONV_CH), lambda bi, j: (bi, j, 0)),
        out_shape=jax.ShapeDtypeStruct((bsz, s, CONV_CH), BF16),
        scratch_shapes=[pltpu.VMEM((ts + CONV_HALO, CONV_CH), F32)],
        compiler_params=pltpu.CompilerParams(
            dimension_semantics=("arbitrary", "arbitrary"), vmem_limit_bytes=VMEM_LIMIT),
        name="conv_module",
    )(cacg, w, b, g, beta)


def _key_to_float(key):
    bits = key ^ ((key >> 31) & jnp.int32(0x7FFFFFFF))
    return lax.bitcast_convert_type(bits, F32)


def _float_to_key(v):
    bits = lax.bitcast_convert_type(v, jnp.int32)
    return bits ^ ((bits >> 31) & jnp.int32(0x7FFFFFFF))


KEY_NEG_INF = int(np.int32(np.uint32(0xFF800000)) ^ np.int32(0x7FFFFFFF))


def _dsa_kernel(hb_ref, q_ref, k_ref, vt_ref, iq_ref, ik_ref, misc_ref, nb_ref, o_ref,
                sc_ref, qm_ref, acc_ref, thr_ref, l_ref, *, topk):
    i = pl.program_id(1)
    s_len = k_ref.shape[1]
    q0 = i * QB
    lane = lax.broadcasted_iota(jnp.int32, (LANES, QB), 1)
    row = lax.broadcasted_iota(jnp.int32, (LANES, QB), 0)
    left = lax.broadcasted_iota(jnp.int32, (QB, LANES), 1) < HEAD_DIM
    qpos = q0 + lane

    iq = iq_ref[0]
    misc_t = misc_ref[0].T
    iqm, iw = [], []
    for h in range(N_IDX_HEADS):
        pair = iq[:, (h // 2) * LANES:(h // 2 + 1) * LANES]
        iqm.append(jnp.where(left if h % 2 == 0 else ~left, pair, jnp.zeros_like(pair)))
        iw.append(misc_t[MISC_IW + h:MISC_IW + h + 1, :] * (IDX_DIM ** -0.5 * N_IDX_HEADS ** -0.5))

    def fold(x, op):
        return op(x.reshape(LANES // SUBLANES, SUBLANES, QB), axis=0)

    def score_chunk(c, carry):
        gmax, n_ge0, n_gt0 = carry
        gmax = list(gmax)
        k0 = pl.multiple_of(c * KC, KC)
        ikc = ik_ref[0, pl.ds(k0, KC), :]
        s = jnp.zeros((KC, QB), F32)
        for h in range(N_IDX_HEADS):
            s = s + jnp.maximum(_dot_nt(ikc, iqm[h]), 0.0) * iw[h]
        for t in range(KC // LANES):
            kpos = k0 + t * LANES + row
            blk = jnp.where(kpos <= qpos, s[t * LANES:(t + 1) * LANES, :], -jnp.inf)
            sc_ref[c * (KC // LANES) + t] = blk
            g = t % (GROUPS // LANES)
            gmax[g] = jnp.maximum(gmax[g], blk)
            n_ge0 = n_ge0 + fold(jnp.where(blk >= 0.0, 1.0, 0.0), jnp.sum)
            n_gt0 = n_gt0 + fold(jnp.where(blk > 0.0, 1.0, 0.0), jnp.sum)
        return tuple(gmax), n_ge0, n_gt0

    per_chunk = KC // LANES
    n_chunks = (q0 + QB - 1) // KC + 1
    gmax, n_ge0, n_gt0 = lax.fori_loop(
        0, n_chunks, score_chunk,
        (tuple(jnp.full((LANES, QB), -jnp.inf, F32) for _ in range(GROUPS // LANES)),
         jnp.zeros((SUBLANES, QB), F32), jnp.zeros((SUBLANES, QB), F32)))

    def count_blocks(pred):
        def body(c, acc):
            for t in range(per_chunk):
                bk = c * per_chunk + t
                acc = acc + jnp.where(pred(sc_ref[bk], bk), 1.0, 0.0)
            return acc
        acc = lax.fori_loop(0, n_chunks, body, jnp.zeros((LANES, QB), F32))
        return jnp.sum(acc, axis=0, keepdims=True)

    everything = jnp.float32(2 * s_len)
    vmax = jnp.max(functools.reduce(jnp.maximum, gmax), axis=0, keepdims=True)
    glow = jnp.min(functools.reduce(jnp.minimum, gmax), axis=0, keepdims=True)
    at0 = jnp.sum(n_ge0, axis=0, keepdims=True)
    above0 = jnp.sum(n_gt0, axis=0, keepdims=True)
    key_glow = _float_to_key(glow)
    positive = above0 >= topk
    negative = at0 < topk
    lo0 = jnp.where(positive, jnp.maximum(key_glow, 1), key_glow)
    c_lo0 = jnp.where(positive & (key_glow < 1), above0, everything)
    hi0 = jnp.where(negative, 0, _float_to_key(vmax) + 1)
    c_hi0 = jnp.where(negative, at0, 0.0)
    zero_tied = jnp.logical_not(positive | negative)
    closed0 = lo0 + 1 >= hi0
    done0 = zero_tied | closed0
    thr0 = jnp.where(zero_tied, 0, lo0)
    n_ge_init = jnp.where(zero_tied, at0, c_lo0)
    need0 = jnp.where(zero_tied, topk - above0, topk - c_hi0)

    steps_per_check = 4

    def search_cond(state):
        it, _, _, _, _, done, _, _, _ = state
        return (it < 36) & (jnp.min(done) == 0)

    def search_steps(state):
        return lax.fori_loop(0, steps_per_check, lambda _, st: search_step(st), state)

    def search_step(state):
        it, lo, hi, c_lo, c_hi, done, thr_key, n_ge, need = state
        mid = (lo >> 1) + (hi >> 1) + (lo & hi & 1)
        mid_b = jnp.broadcast_to(_key_to_float(mid), (LANES, QB))
        cnt = count_blocks(lambda s, bk: s >= mid_b)
        ok = cnt >= topk
        lo2, c_lo2 = jnp.where(ok, mid, lo), jnp.where(ok, cnt, c_lo)
        hi2, c_hi2 = jnp.where(ok, hi, mid), jnp.where(ok, c_hi, cnt)
        hit = cnt == topk
        newly = (done == 0) & (hit | (lo2 + 1 == hi2))
        thr_key = jnp.where(newly, jnp.where(hit, mid, lo2), thr_key)
        n_ge = jnp.where(newly, jnp.where(hit, float(topk), c_lo2), n_ge)
        need = jnp.where(newly, jnp.where(hit, everything, topk - c_hi2), need)
        frozen = done == 1
        return (it + 1, jnp.where(frozen, lo, lo2), jnp.where(frozen, hi, hi2),
                jnp.where(frozen, c_lo, c_lo2), jnp.where(frozen, c_hi, c_hi2),
                jnp.where(newly, 1, done), thr_key, n_ge, need)

    state = lax.while_loop(
        search_cond, search_steps,
        (jnp.int32(0), lo0, hi0, c_lo0, c_hi0, done0.astype(jnp.int32), thr0, n_ge_init, need0))
    thr = _key_to_float(state[6])
    n_ge, need = state[7], state[8]
    thr_b = jnp.broadcast_to(thr, (LANES, QB))
    thr_ref[...] = thr_b

    @pl.when(jnp.max(n_ge) > topk)
    def _():
        rank_mat = jnp.where(
            lax.broadcasted_iota(jnp.int32, (LANES, LANES), 0)
            >= lax.broadcasted_iota(jnp.int32, (LANES, LANES), 1), 1.0, 0.0).astype(BF16)

        def demote(c, seen):
            for t in range(per_chunk):
                bk = c * per_chunk + t
                s = sc_ref[bk]
                tie = s == thr_b
                ind = jnp.where(tie, 1.0, 0.0)
                rank = seen + _dot(rank_mat, ind.astype(BF16))
                sc_ref[bk] = jnp.where(tie & (rank > need), -jnp.inf, s)
                seen = seen + jnp.sum(ind, axis=0, keepdims=True)
            return seen

        lax.fori_loop(0, n_chunks, demote, jnp.zeros((1, QB), F32))

    q = q_ref[0]
    for pr in range(N_HEADS // 2):
        pair = q[:, pr * LANES:(pr + 1) * LANES]
        zero = jnp.zeros_like(pair)
        qm_ref[pr] = jnp.concatenate([jnp.where(left, pair, zero), jnp.where(left, zero, pair)], axis=0)
    odd =lax.broadcasted_iota(jnp.int32, (1, 2 * QB), 1) >= QB

    def mask_tile(bk, active, causal):
        t = jnp.where(active, thr_ref[...], jnp.inf)
        sel = sc_ref[bk] >= t
        if causal:
            sel = sel & (bk * LANES + row <= qpos)
        return jnp.where(sel, 0.0, NEG)

    def pair_row(values, pr):
        return jnp.where(odd, values[2 * pr + 1], values[2 * pr])

    def attend(blocks, maskadd, near, state, robust):
        width = len(blocks) * LANES
        k0 = pl.multiple_of(blocks[0] * LANES, LANES)
        maskadd2 = jnp.concatenate([maskadd, maskadd], axis=1)
        far = [hb_ref[0, h] for h in range(N_HEADS)]
        bound = [hb_ref[1, h] for h in range(N_HEADS)]
        new_state = []
        for pr in range(N_HEADS // 2):
            kp = k_ref[0, pl.ds(k0, width), pr * LANES:(pr + 1) * LANES]
            lg = _dot_nt(kp, qm_ref[pr]) + maskadd2
            if near is None:
                bias = pair_row(far, pr)
            else:
                lg = lg + jnp.concatenate(
                    [nb_ref[2 * pr + hh, pl.ds(near, width), :] for hh in range(2)], axis=1)
                bias = 0.0
            if robust:
                m_old, l_old = state[pr]
                m_new = jnp.maximum(m_old, jnp.max(lg, axis=0, keepdims=True) + bias)
                alpha = jnp.exp(m_old - m_new)
                p = jnp.exp(lg - (m_new - bias))
                new_state.append((m_new, alpha * l_old + jnp.sum(p, axis=0, keepdims=True)))
            else:
                alpha = None
                p = jnp.exp(lg + (bias - pair_row(bound, pr)))
                new_state.append(state[pr] + jnp.sum(p, axis=0, keepdims=True))
            rows = slice(pr * LANES, (pr + 1) * LANES)
            vt = jnp.concatenate([vt_ref[0, bk, rows, :] for bk in blocks], axis=1)
            pv = _dot(vt, p.astype(BF16))
            for hh in range(2):
                hs = slice(pr * LANES + hh * HEAD_DIM, pr * LANES + (hh + 1) * HEAD_DIM)
                cs = slice(hh * QB, (hh + 1) * QB)
                upd = pv[hh * HEAD_DIM:(hh + 1) * HEAD_DIM, cs]
                acc_ref[hs, :] = acc_ref[hs, :] + upd if alpha is None else alpha[:, cs] * acc_ref[hs, :] + upd
        return new_state

    far_end = jnp.maximum(q0 - LANES, 0)
    window = [far_end // LANES + t for t in range(1 + QB // LANES)]
    tile_row = pl.multiple_of(jnp.where(i == 0, LANES, 0), LANES)

    def attention(robust):
        acc_ref[...] = jnp.zeros(acc_ref.shape, F32)
        zero = jnp.zeros((1, 2 * QB), F32)
        state = [(jnp.full((1, 2 * QB), NEG, F32), zero) if robust else zero
                 for _ in range(N_HEADS // 2)]

        def far_chunk(c, state):
            blocks = [c * per_chunk + t for t in range(per_chunk)]
            maskadd = jnp.concatenate(
                [mask_tile(bk, bk * LANES < far_end, False) for bk in blocks], axis=0)
            return attend(blocks, maskadd, None, state, robust)

        state = lax.fori_loop(0, (far_end + KC - 1) // KC, far_chunk, state)
        state = attend(window, jnp.concatenate([mask_tile(bk, True, True) for bk in window], axis=0),
                       tile_row, state, robust)
        for pr in range(N_HEADS // 2):
            l_ref[pr:pr + 1, :] = state[pr][1] if robust else state[pr]

    attention(robust=False)

    @pl.when(jnp.logical_not(jnp.min(l_ref[...]) > DENOM_FLOOR))
    def _():
        attention(robust=True)

    out_t = jnp.concatenate(
        [acc_ref[h * HEAD_DIM:(h + 1) * HEAD_DIM, :]
         / l_ref[h // 2:h // 2 + 1, (h % 2) * QB:(h % 2 + 1) * QB] for h in range(N_HEADS)],
        axis=0)
    o_ref[0] = out_t.T.astype(BF16)


def _dsa(head_scalars, q, k, vt, iq, ik2, misc, near_bias):
    bsz, s, _ = q.shape
    topk = min(TOPK_MAX, s // 4)
    nq = s // QB
    blk = lambda w: pl.BlockSpec((1, QB, w), lambda bi, i: (bi, i, 0))
    once = pl.Buffered(1)
    full = lambda w: pl.BlockSpec((1, s, w), lambda bi, i: (bi, 0, 0), pipeline_mode=once)
    return pl.pallas_call(
        functools.partial(_dsa_kernel, topk=topk),
        grid=(bsz, nq),
        in_specs=[pl.BlockSpec(memory_space=pltpu.SMEM),
                  blk(ATT_W), full(ATT_W),
                  pl.BlockSpec((1, s // LANES, ATT_W, LANES), lambda bi, i: (bi, 0, 0, 0),
                               pipeline_mode=once),
                  blk(N_IDX_HEADS * IDX_DIM), full(LANES), blk(LANES),
                  pl.BlockSpec((N_HEADS, 2 * LANES + QB, QB), lambda bi, i: (0, 0, 0),
                               pipeline_mode=once)],
        out_specs=blk(ATT_W),
        out_shape=jax.ShapeDtypeStruct((bsz, s, ATT_W), BF16),
        scratch_shapes=[
            pltpu.VMEM((_round_up(s, KC) // LANES, LANES, QB), F32),
            pltpu.VMEM((N_HEADS // 2, 2 * QB, LANES), BF16),
            pltpu.VMEM((ATT_W, QB), F32),
            pltpu.VMEM((LANES, QB), F32),
            pltpu.VMEM((N_HEADS // 2, 2 * QB), F32),
        ],
        compiler_params=pltpu.CompilerParams(
            dimension_semantics=("arbitrary", "arbitrary"), vmem_limit_bytes=VMEM_LIMIT),
        name="dsa",
    )(head_scalars, q, k, vt, iq, ik2, misc, near_bias)


def _round_up(a, b):
    return (a + b - 1) // b * b


def _ssd_kernel(xbc_ref, z_ref, misc_ref, cw_ref, cb_ref, dtb_ref, alog_ref, dsk_ref, gn_ref,
                o_ref, ext_ref, st_ref):
    ts = o_ref.shape[1]
    j = pl.program_id(1)

    @pl.when(j == 0)
    def _():
        ext_ref[0:HALO, :] = jnp.zeros((HALO, SSM_XBC), F32)
        st_ref[...] = jnp.zeros(st_ref.shape, F32)

    @pl.when(j > 0)
    def _():
        ext_ref[0:HALO, :] = ext_ref[ts:ts + HALO, :]

    ext_ref[HALO:, :] = xbc_ref[0].astype(F32)

    lane = lax.broadcasted_iota(jnp.int32, (CHUNK, LANES), 1)
    row = lax.broadcasted_iota(jnp.int32, (CHUNK, LANES), 0)
    tril = row >= lane
    tril_b = jnp.where(tril, 1.0, 0.0).astype(BF16)
    is_dt = (lane >= MISC_DT) & (lane < MISC_DT + SSM_HEADS)
    e_row = lax.broadcasted_iota(jnp.int32, (LANES, SSM_INNER), 0)
    e_col = lax.broadcasted_iota(jnp.int32, (LANES, SSM_INNER), 1)
    expand = jnp.where(e_row - MISC_DT == e_col // SSM_HEAD_DIM, 1.0, 0.0).astype(BF16)
    left = lane < SSM_HEAD_DIM
    lane1 = lax.broadcasted_iota(jnp.int32, (1, LANES), 1)
    a_row = jnp.where((lane1 >= MISC_DT) & (lane1 < MISC_DT + SSM_HEADS), -jnp.exp(alog_ref[...]), 0.0)
    cw = cw_ref[...]
    off = HALO - (SSM_CONV_K - 1)
    gw = SSM_GROUPS * SSM_STATE

    for c in range(ts // CHUNK):
        r0 = c * CHUNK
        conv = jnp.broadcast_to(cb_ref[...], (CHUNK, SSM_XBC))
        for tap in range(SSM_CONV_K):
            conv = conv + cw[tap:tap + 1, :] * ext_ref[r0 + off + tap:r0 + off + tap + CHUNK, :]
        xc = _silu(conv)
        xs = xc[:, 0:SSM_INNER]

        dt = jnp.where(is_dt, jax.nn.softplus(misc_ref[0, r0:r0 + CHUNK, :] + dtb_ref[...]), 0.0)
        cs = _cumsum_rows(tril_b, dt * a_row)
        cs_last = cs[CHUNK - 1:CHUNK, :]
        dt_e = _dot_split(dt, expand)
        dec_e = _dot_split(jnp.exp(cs_last - cs), expand)
        ecs_e = _dot_split(jnp.exp(cs), expand)
        cs_t = cs.T

        xd = xs * dt_e
        xd_b = xd.astype(BF16)
        xdd_b = (xd * dec_e).astype(BF16)
        y = dsk_ref[...] * xs
        st_old = st_ref[...]
        st_b = st_old.astype(BF16)
        new_states = []
        y_parts = []
        for g in range(SSM_GROUPS):
            bm = xc[:, SSM_INNER + g * SSM_STATE:SSM_INNER + (g + 1) * SSM_STATE]
            cm_b = xc[:, SSM_INNER + gw + g * SSM_STATE:SSM_INNER + gw + (g + 1) * SSM_STATE].astype(BF16)
            cb = _dot_nt(cm_b, bm.astype(BF16))
            hpg = SSM_HEADS // SSM_GROUPS
            gl = slice(g * hpg * SSM_HEAD_DIM, (g + 1) * hpg * SSM_HEAD_DIM)
            y_off = _dot(cm_b, st_b[:, gl]) * ecs_e[:, gl]
            new_states.append(_dot(bm.T.astype(BF16), xdd_b[:, gl]))
            diag = []
            for pr in range(hpg // 2):
                xp = xd_b[:, gl][:, pr * LANES:(pr + 1) * LANES]
                res = []
                for hh in range(2):
                    hd = g * hpg + 2 * pr + hh
                    seg = cs[:, MISC_DT + hd:MISC_DT + hd + 1] - cs_t[MISC_DT + hd:MISC_DT + hd + 1, :]
                    decay = jnp.exp(jnp.where(tril, seg, -jnp.inf))
                    res.append(_dot((cb * decay).astype(BF16), xp))
                diag.append(jnp.where(left, res[0], res[1]))
            y_parts.append(jnp.concatenate(diag, axis=-1) + y_off)
        y = y + jnp.concatenate(y_parts, axis=-1)
        st_ref[...] = st_old * ecs_e[CHUNK - 1:CHUNK, :] + jnp.concatenate(new_states, axis=-1)

        yz = y * _silu(z_ref[0, r0:r0 + CHUNK, :].astype(F32))
        yn = yz * lax.rsqrt(jnp.mean(yz * yz, axis=-1, keepdims=True) + EPS) * gn_ref[...]
        o_ref[0, r0:r0 + CHUNK, :] = yn.astype(BF16)


def _cumsum_rows(tril_b, x):
    hi = x.astype(BF16)
    r1 = x - hi.astype(F32)
    mid = r1.astype(BF16)
    lo = (r1 - mid.astype(F32)).astype(BF16)
    return _dot(tril_b, hi) + _dot(tril_b, mid) + _dot(tril_b, lo)


def _ssd(xbc, z, misc, cw, cb, dtb, alog, dsk, gn, ts):
    bsz, s, _ = xbc.shape
    blk = lambda w: pl.BlockSpec((1, ts, w), lambda bi, j: (bi, j, 0))
    return pl.pallas_call(
        _ssd_kernel,
        grid=(bsz, s // ts),
        in_specs=[blk(SSM_XBC), blk(SSM_INNER), blk(LANES),
                  _const_spec((HALO, SSM_XBC)), _const_spec((1, SSM_XBC)),
                  _const_spec((1, LANES)), _const_spec((1, LANES)),
                  _const_spec((1, SSM_INNER)), _const_spec((1, SSM_INNER))],
        out_specs=blk(SSM_INNER),
        out_shape=jax.ShapeDtypeStruct((bsz, s, SSM_INNER), BF16),
        scratch_shapes=[pltpu.VMEM((ts + HALO, SSM_XBC), F32),
                        pltpu.VMEM((SSM_STATE, SSM_INNER), F32)],
        compiler_params=pltpu.CompilerParams(
            dimension_semantics=("arbitrary", "arbitrary"), vmem_limit_bytes=VMEM_LIMIT),
        name="ssd",
    )(xbc, z, misc, cw, cb, dtb, alog, dsk, gn)


FF_CHUNK = D_FF // 2


def _ffn_kernel(x_ref, u_ref, att_ref, y_ref, wo_ref, g_ref, wup_ref, cw_ref, cb_ref, wdn_ref,
                o_ref, ext_a, ext_g, carry_ref, gated_ref):
    tm = x_ref.shape[1]
    j = pl.program_id(1)

    @pl.when(j == 0)
    def _():
        carry_ref[...] = jnp.zeros(carry_ref.shape, F32)

    x1 = (x_ref[0]
          + _dot(u_ref[0], wo_ref[0:CONV_CH, :])
          + _dot(att_ref[0], wo_ref[CONV_CH:CONV_CH + ATT_W, :])
          + _dot(y_ref[0], wo_ref[CONV_CH + ATT_W:D_MIX, :]))
    hb = (x1 * lax.rsqrt(jnp.mean(x1 * x1, axis=-1, keepdims=True) + EPS) * g_ref[...]).astype(BF16)

    off = HALO - (FFN_CONV_K - 1)

    def conv_half(ext, c0):
        ext[0:HALO, :] = carry_ref[:, c0:c0 + FF_CHUNK]
        ext[HALO:, :] = _dot(hb, wup_ref[:, c0:c0 + FF_CHUNK])
        carry_ref[:, c0:c0 + FF_CHUNK] = ext[tm:tm + HALO, :]
        out = jnp.broadcast_to(cb_ref[:, c0:c0 + FF_CHUNK], (tm, FF_CHUNK))
        for tap in range(FFN_CONV_K):
            out = out + cw_ref[tap:tap + 1, c0:c0 + FF_CHUNK] * ext[off + tap:off + tap + tm, :]
        return out

    for c in range(D_FF // FF_CHUNK):
        fa = conv_half(ext_a, c * FF_CHUNK)
        fg = conv_half(ext_g, D_FF + c * FF_CHUNK)
        gated_ref[:, c * FF_CHUNK:(c + 1) * FF_CHUNK] = (_silu(fg) * fa).astype(BF16)
    o_ref[0] = x1 + _dot(gated_ref[...], wdn_ref[...])


def _ffn(x, u, att, y, wo, g, wup, cw, cb, wdn, tm):
    bsz, s, d = x.shape
    blk = lambda w: pl.BlockSpec((1, tm, w), lambda bi, j: (bi, j, 0))
    single = lambda shape: pl.BlockSpec(shape, lambda *_: (0,) * len(shape),
                                        pipeline_mode=pl.Buffered(1))
    return pl.pallas_call(
        _ffn_kernel,
        grid=(bsz, s // tm),
        in_specs=[blk(d), blk(CONV_CH), blk(ATT_W), blk(SSM_INNER),
                  single((D_MIX, d)), _const_spec((1, d)),
                  single((d, 2 * D_FF)), _const_spec((HALO, 2 * D_FF)), _const_spec((1, 2 * D_FF)),
                  single((D_FF, d))],
        out_specs=blk(d),
        out_shape=jax.ShapeDtypeStruct((bsz, s, d), F32),
        scratch_shapes=[pltpu.VMEM((tm + HALO, FF_CHUNK), F32),
                        pltpu.VMEM((tm + HALO, FF_CHUNK), F32),
                        pltpu.VMEM((HALO, 2 * D_FF), F32),
                        pltpu.VMEM((tm, D_FF), BF16)],
        input_output_aliases={0: 0},
        compiler_params=pltpu.CompilerParams(
            dimension_semantics=("arbitrary", "arbitrary"), vmem_limit_bytes=VMEM_LIMIT),
        name="ffn",
    )(x, u, att, y, wo, g, wup, cw, cb, wdn)


def _t5_bucket(rel):
    n = jnp.maximum(rel, 0)
    max_exact = NUM_BUCKETS // 2
    nf = jnp.maximum(n, 1).astype(F32)
    large = max_exact + (jnp.log(nf / max_exact) / math.log(MAX_DISTANCE / max_exact)
                         * (NUM_BUCKETS - max_exact)).astype(jnp.int32)
    large = jnp.minimum(large, NUM_BUCKETS - 1)
    return jnp.where(n < max_exact, n, large)


def _bias_tables(rel_bias):
    kk = jnp.arange(2 * LANES + QB)[:, None]
    tq = jnp.arange(QB)[None, :]
    rel = tq + LANES - kk
    onehot = (_t5_bucket(rel)[..., None] == jnp.arange(NUM_BUCKETS)).astype(F32)
    near = jnp.einsum("kqb,bh->hkq", onehot, rel_bias.astype(F32),
                      precision=lax.Precision.HIGHEST)
    far = rel_bias[_t5_bucket(jnp.int32(LANES + 1))].astype(F32)
    return near, far


def _pad_rows(w, rows):
    return jnp.concatenate([w, jnp.zeros((w.shape[0], rows - w.shape[1], w.shape[2]), w.dtype)], axis=1)


def _pad_lanes(v, start, width=LANES):
    out = jnp.zeros(v.shape[:-1] + (width,), v.dtype)
    return out.at[..., start:start + v.shape[-1]].set(v)


def kernel(x, rel_bias, norm_mix_g, w_in, conv_dw_w, conv_dw_b, conv_ln_g, conv_ln_b, q_norm_g, k_norm_g, ssm_conv_w, ssm_conv_b, dt_bias, a_log, d_skip, ssm_norm_g, w_out, norm_ffn_g, w_up, ffn_conv_w, ffn_conv_b, w_down):
    bsz, s, d = x.shape
    depth = w_in.shape[0]
    assert s % KC == 0 and s >= 2 * QB

    sizes = [CONV_CH, CONV_CH, ATT_W, ATT_W, ATT_W, N_IDX_HEADS * IDX_DIM, IDX_DIM, N_IDX_HEADS,
             SSM_INNER, SSM_XBC, SSM_HEADS]
    splits = [int(v) for v in np.cumsum(sizes)[:-1]]
    ca, cg, wq, wk, wv, wiq, wik, wiw, wz, wxbc, wdt = jnp.split(w_in, splits, axis=-1)
    w_main = jnp.concatenate([ca, cg, wq, wk, wiq, wik, wik, wz, wxbc], axis=-1).astype(BF16)
    w_vt = jnp.swapaxes(wv, 1, 2).astype(BF16)
    w_misc = jnp.concatenate(
        [_pad_lanes(wiw, 0, MISC_DT), _pad_lanes(wdt, 0, LANES - MISC_DT)], axis=-1).astype(BF16)
    seg = jnp.asarray(np.kron(np.eye(N_HEADS), np.full((HEAD_DIM, HEAD_DIM), 1.0 / HEAD_DIM)), BF16)

    near_bias, far_bias = _bias_tables(rel_bias)
    logit_bound = (BOUND_SLACK * HEAD_DIM ** 0.5
                   * jnp.max(jnp.abs(q_norm_g), axis=-1, keepdims=True)
                   * jnp.max(jnp.abs(k_norm_g), axis=-1, keepdims=True)
                   + jnp.max(rel_bias, axis=0)[None, :]).astype(F32)
    head_scalars = jnp.stack([jnp.broadcast_to(far_bias, logit_bound.shape), logit_bound], axis=1)

    layer_params = dict(
        g_mix=norm_mix_g[:, None, :], w_main=w_main, w_misc=w_misc, w_vt=w_vt,
        head_scalars=head_scalars,
        gq=jnp.tile(q_norm_g, (1, N_HEADS))[:, None, :], gk=jnp.tile(k_norm_g, (1, N_HEADS))[:, None, :],
        conv_w=_pad_rows(conv_dw_w, CONV_HALO), conv_b=conv_dw_b[:, None, :],
        ln_g=conv_ln_g[:, None, :], ln_b=conv_ln_b[:, None, :],
        ssm_w=_pad_rows(ssm_conv_w, HALO), ssm_b=ssm_conv_b[:, None, :],
        dtb=_pad_lanes(dt_bias, MISC_DT)[:, None, :], alog=_pad_lanes(a_log, MISC_DT)[:, None, :],
        dsk=jnp.repeat(d_skip, SSM_HEAD_DIM, axis=-1)[:, None, :], gn=ssm_norm_g[:, None, :],
        w_out=w_out.astype(BF16), g_ffn=norm_ffn_g[:, None, :], w_up=w_up.astype(BF16),
        ffn_w=_pad_rows(ffn_conv_w, HALO), ffn_b=ffn_conv_b[:, None, :], w_down=w_down.astype(BF16),
    )

    tm = min(512, s)

    def layer(xc, p):
        (cacg, q, k, vt, iq, ik2, z, xbc, misc) = _in_proj(
            xc.reshape(bsz * s, d), p["g_mix"], p["w_main"], p["w_misc"], p["w_vt"], p["gq"], p["gk"],
            seg, tm)
        r3 = lambda a: a.reshape(bsz, s, a.shape[-1])
        u = _conv_module(r3(cacg), p["conv_w"], p["conv_b"], p["ln_g"], p["ln_b"], tm)
        att = _dsa(p["head_scalars"], r3(q), r3(k), vt.reshape(bsz, s // LANES, ATT_W, LANES), r3(iq), r3(ik2),
                   r3(misc), near_bias)
        y = _ssd(r3(xbc), r3(z), r3(misc), p["ssm_w"], p["ssm_b"], p["dtb"], p["alog"],
                 p["dsk"], p["gn"], tm)
        xn = _ffn(xc, u, att, y, p["w_out"], p["g_ffn"], p["w_up"], p["ffn_w"], p["ffn_b"],
                  p["w_down"], tm)
        return xn, None

    out, _ = lax.scan(layer, x, layer_params)
    return out
```

```python
import functools
import math

import jax
import jax.numpy as jnp
import numpy as np
from jax import lax
from jax.experimental import pallas as pl
from jax.experimental.pallas import tpu as pltpu

F32 = jnp.float32
BF16 = jnp.bfloat16

EPS = 1e-6
LANES = 128
SUBLANES = 8
CONV_CH = 512
CONV_K = 31
N_HEADS = 8
HEAD_DIM = 64
ATT_W = N_HEADS * HEAD_DIM
N_IDX_HEADS = 4
IDX_DIM = 64
TOPK_MAX = 256
NUM_BUCKETS = 32
MAX_DISTANCE = 128
SSM_HEADS = 8
SSM_HEAD_DIM = 64
SSM_INNER = SSM_HEADS * SSM_HEAD_DIM
SSM_GROUPS = 2
SSM_STATE = 128
SSM_CONV_K = 4
SSM_XBC = SSM_INNER + 2 * SSM_GROUPS * SSM_STATE
CHUNK = 128
D_MIX = CONV_CH + ATT_W + SSM_INNER
D_FF = 2816
FFN_CONV_K = 3

MISC_IW = 0
MISC_DT = 8
HALO = 8
CONV_HALO = 32
QB = 512
KC = 512
GROUPS = 256
NEG = -1e30
DENOM_FLOOR = 1e-25
BOUND_SLACK = 1.03
VMEM_LIMIT = 56 * 1024 * 1024

NT_DIMS = (((1,), (1,)), ((), ()))


def _dot(a, b):
    return jnp.dot(a, b, preferred_element_type=F32)


def _dot_nt(a, b):
    return lax.dot_general(a, b, NT_DIMS, preferred_element_type=F32)


def _dot_split(x, m_bf16):
    hi = x.astype(BF16)
    r1 = x - hi.astype(F32)
    mid = r1.astype(BF16)
    lo = (r1 - mid.astype(F32)).astype(BF16)
    return _dot(hi, m_bf16) + _dot(mid, m_bf16) + _dot(lo, m_bf16)


def _silu(x):
    return x * jax.nn.sigmoid(x)


def _const_spec(shape):
    n = len(shape)
    return pl.BlockSpec(shape, lambda *_: (0,) * n)


def _in_proj_kernel(x_ref, g_ref, w_ref, wm_ref, wvt_ref, gq_ref, gk_ref, seg_ref,
                    cacg_ref, q_ref, k_ref, vt_ref, iq_ref, ik_ref, z_ref, xbc_ref, misc_ref):
    x = x_ref[...]
    h = x * lax.rsqrt(jnp.mean(x * x, axis=-1, keepdims=True) + EPS) * g_ref[...]
    hb = h.astype(BF16)

    def proj(c0, width):
        return _dot(hb, w_ref[:, c0:c0 + width])

    cacg_ref[...] = proj(0, 1024).astype(BF16)

    def qk_norm(c0, gain_ref, scale):
        t = proj(c0, ATT_W)
        ms = _dot((t * t).astype(BF16), seg_ref[...])
        return (t * lax.rsqrt(ms + EPS) * (gain_ref[...] * scale)).astype(BF16)

    q_ref[...] = qk_norm(1024, gq_ref, HEAD_DIM ** -0.5)
    k_ref[...] = qk_norm(1536, gk_ref, 1.0)
    vt = _dot_nt(wvt_ref[...], hb).astype(BF16)
    for j in range(vt_ref.shape[0]):
        vt_ref[j] = vt[:, j * LANES:(j + 1) * LANES]
    iq_ref[...] = proj(2048, N_IDX_HEADS * IDX_DIM).astype(BF16)
    ik_ref[...] = proj(2304, LANES).astype(BF16)
    z_ref[...] = proj(2432, SSM_INNER).astype(BF16)
    xbc_ref[...] = proj(2944, SSM_XBC).astype(BF16)
    misc_ref[...] = _dot(hb, wm_ref[...])


W_MAIN_COLS = 2944 + SSM_XBC


def _in_proj(x2, g, w_main, w_misc, w_vt, gq, gk, seg, tm):
    t, d = x2.shape
    rows = lambda w, dt: (pl.BlockSpec((tm, w), lambda i: (i, 0)), jax.ShapeDtypeStruct((t, w), dt))
    vt_out = (pl.BlockSpec((tm // LANES, ATT_W, LANES), lambda i: (i, 0, 0)),
              jax.ShapeDtypeStruct((t // LANES, ATT_W, LANES), BF16))
    outs = [rows(1024, BF16), rows(ATT_W, BF16), rows(ATT_W, BF16), vt_out,
            rows(N_IDX_HEADS * IDX_DIM, BF16), rows(LANES, BF16), rows(SSM_INNER, BF16),
            rows(SSM_XBC, BF16), rows(LANES, F32)]
    return pl.pallas_call(
        _in_proj_kernel,
        grid=(t // tm,),
        in_specs=[pl.BlockSpec((tm, d), lambda i: (i, 0)),
                  _const_spec((1, d)),
                  _const_spec((d, W_MAIN_COLS)),
                  _const_spec((d, LANES)),
                  _const_spec((ATT_W, d)),
                  _const_spec((1, ATT_W)), _const_spec((1, ATT_W)),
                  _const_spec((ATT_W, ATT_W))],
        out_specs=[spec for spec, _ in outs],
        out_shape=[shape for _, shape in outs],
        compiler_params=pltpu.CompilerParams(
            dimension_semantics=("arbitrary",), vmem_limit_bytes=VMEM_LIMIT),
        name="in_proj",
    )(x2, g, w_main, w_misc, w_vt, gq, gk, seg)


CONV_ROWS = 64


def _conv_kernel(cacg_ref, w_ref, b_ref, g_ref, beta_ref, o_ref, ext_ref):
    ts = o_ref.shape[1]
    j = pl.program_id(1)

    @pl.when(j == 0)
    def _():
        ext_ref[0:CONV_HALO, :] = jnp.zeros((CONV_HALO, CONV_CH), F32)

    @pl.when(j > 0)
    def _():
        ext_ref[0:CONV_HALO, :] = ext_ref[ts:ts + CONV_HALO, :]

    ca = cacg_ref[0, :, 0:CONV_CH].astype(F32)
    cg = cacg_ref[0, :, CONV_CH:2 * CONV_CH].astype(F32)
    ext_ref[CONV_HALO:, :] = ca * jax.nn.sigmoid(cg)

    w = w_ref[...]
    off = CONV_HALO - (CONV_K - 1)
    for r in range(ts // CONV_ROWS):
        base = r * CONV_ROWS
        acc = jnp.broadcast_to(b_ref[...], (CONV_ROWS, CONV_CH))
        for res in range(SUBLANES):
            rows = CONV_ROWS + (SUBLANES if res else 0)
            part = None
            for tap in range(CONV_K):
                if (off + tap) % SUBLANES != res:
                    continue
                start = base + off + tap - res
                term = w[tap:tap + 1, :] * ext_ref[start:start + rows, :]
                part = term if part is None else part + term
            acc = acc + part[res:res + CONV_ROWS, :]
        mu = jnp.mean(acc, axis=-1, keepdims=True)
        cen = acc - mu
        var = jnp.mean(cen * cen, axis=-1, keepdims=True)
        yn = cen * lax.rsqrt(var + EPS) * g_ref[...] + beta_ref[...]
        o_ref[0, base:base + CONV_ROWS, :] = _silu(yn).astype(BF16)


def _conv_module(cacg, w, b, g, beta, ts):
    bsz, s, _ = cacg.shape
    return pl.pallas_call(
        _conv_kernel,
        grid=(bsz, s // ts),
        in_specs=[pl.BlockSpec((1, ts, 2 * CONV_CH), lambda bi, j: (bi, j, 0)),
                  _const_spec((CONV_HALO, CONV_CH)),
                  _const_spec((1, CONV_CH)), _const_spec((1, CONV_CH)), _const_spec((1, CONV_CH))],
        out_specs=pl.BlockSpec((1, ts, CONV_CH), lambda bi, j: (bi, j, 0)),
        out_shape=jax.ShapeDtypeStruct((bsz, s, CONV_CH), BF16),
        scratch_shapes=[pltpu.VMEM((ts + CONV_HALO, CONV_CH), F32)],
        compiler_params=pltpu.CompilerParams(
            dimension_semantics=("arbitrary", "arbitrary"), vmem_limit_bytes=VMEM_LIMIT),
        name="conv_module",
    )(cacg, w, b, g, beta)


def _key_to_float(key):
    bits = key ^ ((key >> 31) & jnp.int32(0x7FFFFFFF))
    return lax.bitcast_convert_type(bits, F32)


def _float_to_key(v):
    bits = lax.bitcast_convert_type(v, jnp.int32)
    return bits ^ ((bits >> 31) & jnp.int32(0x7FFFFFFF))


KEY_NEG_INF = int(np.int32(np.uint32(0xFF800000)) ^ np.int32(0x7FFFFFFF))


def _dsa_kernel(hb_ref, q_ref, k_ref, vt_ref, iq_ref, ik_ref, misc_ref, nb_ref, o_ref,
                sc_ref, qm_ref, acc_ref, thr_ref, l_ref, *, topk):
    i = pl.program_id(1)
    s_len = k_ref.shape[1]
    q0 = i * QB
    lane = lax.broadcasted_iota(jnp.int32, (LANES, QB), 1)
    row = lax.broadcasted_iota(jnp.int32, (LANES, QB), 0)
    left = lax.broadcasted_iota(jnp.int32, (QB, LANES), 1) < HEAD_DIM
    qpos = q0 + lane

    iq = iq_ref[0]
    misc_t = misc_ref[0].T
    iqm, iw = [], []
    for h in range(N_IDX_HEADS):
        pair = iq[:, (h // 2) * LANES:(h // 2 + 1) * LANES]
        iqm.append(jnp.where(left if h % 2 == 0 else ~left, pair, jnp.zeros_like(pair)))
        iw.append(misc_t[MISC_IW + h:MISC_IW + h + 1, :] * (IDX_DIM ** -0.5 * N_IDX_HEADS ** -0.5))

    def fold(x, op):
        return op(x.reshape(LANES // SUBLANES, SUBLANES, QB), axis=0)

    def score_chunk(c, carry):
        gmax, n_ge0, n_gt0 = carry
        gmax = list(gmax)
        k0 = pl.multiple_of(c * KC, KC)
        ikc = ik_ref[0, pl.ds(k0, KC), :]
        s = jnp.zeros((KC, QB), F32)
        for h in range(N_IDX_HEADS):
            s = s + jnp.maximum(_dot_nt(ikc, iqm[h]), 0.0) * iw[h]
        for t in range(KC // LANES):
            kpos = k0 + t * LANES + row
            blk = jnp.where(kpos <= qpos, s[t * LANES:(t + 1) * LANES, :], -jnp.inf)
            sc_ref[c * (KC // LANES) + t] = blk
            g = t % (GROUPS // LANES)
            gmax[g] = jnp.maximum(gmax[g], blk)
            n_ge0 = n_ge0 + fold(jnp.where(blk >= 0.0, 1.0, 0.0), jnp.sum)
            n_gt0 = n_gt0 + fold(jnp.where(blk > 0.0, 1.0, 0.0), jnp.sum)
        return tuple(gmax), n_ge0, n_gt0

    per_chunk = KC // LANES
    n_chunks = (q0 + QB - 1) // KC + 1
    gmax, n_ge0, n_gt0 = lax.fori_loop(
        0, n_chunks, score_chunk,
        (tuple(jnp.full((LANES, QB), -jnp.inf, F32) for _ in range(GROUPS // LANES)),
         jnp.zeros((SUBLANES, QB), F32), jnp.zeros((SUBLANES, QB), F32)))

    def count_blocks(pred):
        def body(c, acc):
            for t in range(per_chunk):
                bk = c * per_chunk + t
                acc = acc + fold(jnp.where(pred(sc_ref[bk], bk), 1.0, 0.0), jnp.sum)
            return acc
        acc = lax.fori_loop(0, n_chunks, body, jnp.zeros((SUBLANES, QB), F32))
        return jnp.sum(acc, axis=0, keepdims=True)

    everything = jnp.float32(2 * s_len)
    vmax = jnp.max(functools.reduce(jnp.maximum, gmax), axis=0, keepdims=True)
    glow = jnp.min(functools.reduce(jnp.minimum, gmax), axis=0, keepdims=True)
    at0 = jnp.sum(n_ge0, axis=0, keepdims=True)
    above0 = jnp.sum(n_gt0, axis=0, keepdims=True)
    key_glow = _float_to_key(glow)
    positive = above0 >= topk
    negative = at0 < topk
    lo0 = jnp.where(positive, jnp.maximum(key_glow, 1), key_glow)
    c_lo0 = jnp.where(positive & (key_glow < 1), above0, everything)
    hi0 = jnp.where(negative, 0, _float_to_key(vmax) + 1)
    c_hi0 = jnp.where(negative, at0, 0.0)
    zero_tied = jnp.logical_not(positive | negative)
    closed0 = lo0 + 1 >= hi0
    done0 = zero_tied | closed0
    thr0 = jnp.where(zero_tied, 0, lo0)
    n_ge_init = jnp.where(zero_tied, at0, c_lo0)
    need0 = jnp.where(zero_tied, topk - above0, topk - c_hi0)

    steps_per_check = 4

    def search_cond(state):
        it, _, _, _, _, done, _, _, _ = state
        return (it < 36) & (jnp.min(done) == 0)

    def search_steps(state):
        return lax.fori_loop(0, steps_per_check, lambda _, st: search_step(st), state)

    def search_step(state):
        it, lo, hi, c_lo, c_hi, done, thr_key, n_ge, need = state
        mid = (lo >> 1) + (hi >> 1) + (lo & hi & 1)
        mid_b = jnp.broadcast_to(_key_to_float(mid), (LANES, QB))
        cnt = count_blocks(lambda s, bk: s >= mid_b)
        ok = cnt >= topk
        lo2, c_lo2 = jnp.where(ok, mid, lo), jnp.where(ok, cnt, c_lo)
        hi2, c_hi2 = jnp.where(ok, hi, mid), jnp.where(ok, c_hi, cnt)
        hit = cnt == topk
        newly = (done == 0) & (hit | (lo2 + 1 == hi2))
        thr_key = jnp.where(newly, jnp.where(hit, mid, lo2), thr_key)
        n_ge = jnp.where(newly, jnp.where(hit, float(topk), c_lo2), n_ge)
        need = jnp.where(newly, jnp.where(hit, everything, topk - c_hi2), need)
        frozen = done == 1
        return (it + 1, jnp.where(frozen, lo, lo2), jnp.where(frozen, hi, hi2),
                jnp.where(frozen, c_lo, c_lo2), jnp.where(frozen, c_hi, c_hi2),
                jnp.where(newly, 1, done), thr_key, n_ge, need)

    state = lax.while_loop(
        search_cond, search_steps,
        (jnp.int32(0), lo0, hi0, c_lo0, c_hi0, done0.astype(jnp.int32), thr0, n_ge_init, need0))
    thr = _key_to_float(state[6])
    n_ge, need = state[7], state[8]
    thr_b = jnp.broadcast_to(thr, (LANES, QB))
    thr_ref[...] = thr_b

    @pl.when(jnp.max(n_ge) > topk)
    def _():
        rank_mat = jnp.where(
            lax.broadcasted_iota(jnp.int32, (LANES, LANES), 0)
            >= lax.broadcasted_iota(jnp.int32, (LANES, LANES), 1), 1.0, 0.0).astype(BF16)

        def demote(c, seen):
            for t in range(per_chunk):
                bk = c * per_chunk + t
                s = sc_ref[bk]
                tie = s == thr_b
                ind = jnp.where(tie, 1.0, 0.0)
                rank = seen + _dot(rank_mat, ind.astype(BF16))
                sc_ref[bk] = jnp.where(tie & (rank > need), -jnp.inf, s)
                seen = seen + jnp.sum(ind, axis=0, keepdims=True)
            return seen

        lax.fori_loop(0, n_chunks, demote, jnp.zeros((1, QB), F32))

    q = q_ref[0]
    for pr in range(N_HEADS // 2):
        pair = q[:, pr * LANES:(pr + 1) * LANES]
        zero = jnp.zeros_like(pair)
        qm_ref[pr] = jnp.concatenate([jnp.where(left, pair, zero), jnp.where(left, zero, pair)], axis=0)
    odd =lax.broadcasted_iota(jnp.int32, (1, 2 * QB), 1) >= QB

    def mask_tile(bk, active, causal):
        t = jnp.where(active, thr_ref[...], jnp.inf)
        sel = sc_ref[bk] >= t
        if causal:
            sel = sel & (bk * LANES + row <= qpos)
        return jnp.where(sel, 0.0, NEG)

    def pair_row(values, pr):
        return jnp.where(odd, values[2 * pr + 1], values[2 * pr])

    def attend(blocks, maskadd, near, state, robust):
        width = len(blocks) * LANES
        k0 = pl.multiple_of(blocks[0] * LANES, LANES)
        maskadd2 = jnp.concatenate([maskadd, maskadd], axis=1)
        far = [hb_ref[0, h] for h in range(N_HEADS)]
        bound = [hb_ref[1, h] for h in range(N_HEADS)]
        new_state = []
        for pr in range(N_HEADS // 2):
            kp = k_ref[0, pl.ds(k0, width), pr * LANES:(pr + 1) * LANES]
            lg = _dot_nt(kp, qm_ref[pr]) + maskadd2
            if near is None:
                bias = pair_row(far, pr)
            else:
                n_w = len(blocks)
                lg = lg + jnp.concatenate(
                    [jnp.concatenate(
                        [nb_ref[2 * pr + hh, jnp.maximum(qt - w - near + n_w - 1, 0)]
                         for hh in range(2) for qt in range(QB // LANES)], axis=1)
                     for w in range(n_w)], axis=0)
                bias = 0.0
            if robust:
                m_old, l_old = state[pr]
                m_new = jnp.maximum(m_old, jnp.max(lg, axis=0, keepdims=True) + bias)
                alpha = jnp.exp(m_old - m_new)
                p = jnp.exp(lg - (m_new - bias))
                new_state.append((m_new, alpha * l_old + jnp.sum(p, axis=0, keepdims=True)))
            else:
                alpha = None
                p = jnp.exp(lg + (bias - pair_row(bound, pr)))
                new_state.append(state[pr] + jnp.sum(p, axis=0, keepdims=True))
            rows = slice(pr * LANES, (pr + 1) * LANES)
            vt = jnp.concatenate([vt_ref[0, bk, rows, :] for bk in blocks], axis=1)
            pv = _dot(vt, p.astype(BF16))
            for hh in range(2):
                hs = slice(pr * LANES + hh * HEAD_DIM, pr * LANES + (hh + 1) * HEAD_DIM)
                cs = slice(hh * QB, (hh + 1) * QB)
                upd = pv[hh * HEAD_DIM:(hh + 1) * HEAD_DIM, cs]
                acc_ref[hs, :] = acc_ref[hs, :] + upd if alpha is None else alpha[:, cs] * acc_ref[hs, :] + upd
        return new_state

    far_end = jnp.maximum(q0 - LANES, 0)
    window = [far_end // LANES + t for t in range(1 + QB // LANES)]
    window_shift = jnp.where(i == 0, 1, 0)

    def attention(robust):
        acc_ref[...] = jnp.zeros(acc_ref.shape, F32)
        zero = jnp.zeros((1, 2 * QB), F32)
        state = [(jnp.full((1, 2 * QB), NEG, F32), zero) if robust else zero
                 for _ in range(N_HEADS // 2)]

        def far_chunk(c, state):
            blocks = [c * per_chunk + t for t in range(per_chunk)]
            maskadd = jnp.concatenate(
                [mask_tile(bk, bk * LANES < far_end, False) for bk in blocks], axis=0)
            return attend(blocks, maskadd, None, state, robust)

        state = lax.fori_loop(0, (far_end + KC - 1) // KC, far_chunk, state)
        state = attend(window, jnp.concatenate([mask_tile(bk, True, True) for bk in window], axis=0),
                       window_shift, state, robust)
        for pr in range(N_HEADS // 2):
            l_ref[pr:pr + 1, :] = state[pr][1] if robust else state[pr]

    attention(robust=False)

    @pl.when(jnp.logical_not(jnp.min(l_ref[...]) > DENOM_FLOOR))
    def _():
        attention(robust=True)

    out_t = jnp.concatenate(
        [acc_ref[h * HEAD_DIM:(h + 1) * HEAD_DIM, :]
         / l_ref[h // 2:h // 2 + 1, (h % 2) * QB:(h % 2 + 1) * QB] for h in range(N_HEADS)],
        axis=0)
    o_ref[0] = out_t.T.astype(BF16)


def _dsa(head_scalars, q, k, vt, iq, ik2, misc, near_bias):
    bsz, s, _ = q.shape
    topk = min(TOPK_MAX, s // 4)
    nq = s // QB
    blk = lambda w: pl.BlockSpec((1, QB, w), lambda bi, i: (bi, i, 0))
    once = pl.Buffered(1)
    full = lambda w: pl.BlockSpec((1, s, w), lambda bi, i: (bi, 0, 0), pipeline_mode=once)
    return pl.pallas_call(
        functools.partial(_dsa_kernel, topk=topk),
        grid=(bsz, nq),
        in_specs=[pl.BlockSpec(memory_space=pltpu.SMEM),
                  blk(ATT_W), full(ATT_W),
                  pl.BlockSpec((1, s // LANES, ATT_W, LANES), lambda bi, i: (bi, 0, 0, 0),
                               pipeline_mode=once),
                  blk(N_IDX_HEADS * IDX_DIM), full(LANES), blk(LANES),
                  pl.BlockSpec((N_HEADS, 2 * (QB // LANES), LANES, LANES), lambda bi, i: (0, 0, 0, 0),
                               pipeline_mode=once)],
        out_specs=blk(ATT_W),
        out_shape=jax.ShapeDtypeStruct((bsz, s, ATT_W), BF16),
        scratch_shapes=[
            pltpu.VMEM((_round_up(s, KC) // LANES, LANES, QB), F32),
            pltpu.VMEM((N_HEADS // 2, 2 * QB, LANES), BF16),
            pltpu.VMEM((ATT_W, QB), F32),
            pltpu.VMEM((LANES, QB), F32),
            pltpu.VMEM((N_HEADS // 2, 2 * QB), F32),
        ],
        compiler_params=pltpu.CompilerParams(
            dimension_semantics=("arbitrary", "arbitrary"), vmem_limit_bytes=VMEM_LIMIT),
        name="dsa",
    )(head_scalars, q, k, vt, iq, ik2, misc, near_bias)


def _round_up(a, b):
    return (a + b - 1) // b * b


def _ssd_kernel(xbc_ref, z_ref, misc_ref, cw_ref, cb_ref, dtb_ref, alog_ref, dsk_ref, gn_ref,
                o_ref, ext_ref, st_ref):
    ts = o_ref.shape[1]
    j = pl.program_id(1)

    @pl.when(j == 0)
    def _():
        ext_ref[0:HALO, :] = jnp.zeros((HALO, SSM_XBC), F32)
        st_ref[...] = jnp.zeros(st_ref.shape, F32)

    @pl.when(j > 0)
    def _():
        ext_ref[0:HALO, :] = ext_ref[ts:ts + HALO, :]

    ext_ref[HALO:, :] = xbc_ref[0].astype(F32)

    lane = lax.broadcasted_iota(jnp.int32, (CHUNK, LANES), 1)
    row = lax.broadcasted_iota(jnp.int32, (CHUNK, LANES), 0)
    tril = row >= lane
    tril_b = jnp.where(tril, 1.0, 0.0).astype(BF16)
    is_dt = (lane >= MISC_DT) & (lane < MISC_DT + SSM_HEADS)
    e_row = lax.broadcasted_iota(jnp.int32, (LANES, SSM_INNER), 0)
    e_col = lax.broadcasted_iota(jnp.int32, (LANES, SSM_INNER), 1)
    expand = jnp.where(e_row - MISC_DT == e_col // SSM_HEAD_DIM, 1.0, 0.0).astype(BF16)
    left = lane < SSM_HEAD_DIM
    lane1 = lax.broadcasted_iota(jnp.int32, (1, LANES), 1)
    a_row = jnp.where((lane1 >= MISC_DT) & (lane1 < MISC_DT + SSM_HEADS), -jnp.exp(alog_ref[...]), 0.0)
    cw = cw_ref[...]
    off = HALO - (SSM_CONV_K - 1)
    gw = SSM_GROUPS * SSM_STATE

    for c in range(ts // CHUNK):
        r0 = c * CHUNK
        conv = jnp.broadcast_to(cb_ref[...], (CHUNK, SSM_XBC))
        for tap in range(SSM_CONV_K):
            conv = conv + cw[tap:tap + 1, :] * ext_ref[r0 + off + tap:r0 + off + tap + CHUNK, :]
        xc = _silu(conv)
        xs = xc[:, 0:SSM_INNER]

        dt = jnp.where(is_dt, jax.nn.softplus(misc_ref[0, r0:r0 + CHUNK, :] + dtb_ref[...]), 0.0)
        cs = _cumsum_rows(tril_b, dt * a_row)
        cs_last = cs[CHUNK - 1:CHUNK, :]
        dt_e = _dot_split(dt, expand)
        dec_e = _dot_split(jnp.exp(cs_last - cs), expand)
        ecs_e = _dot_split(jnp.exp(cs), expand)
        cs_t = cs.T

        xd = xs * dt_e
        xd_b = xd.astype(BF16)
        xdd_b = (xd * dec_e).astype(BF16)
        y = dsk_ref[...] * xs
        st_old = st_ref[...]
        st_b = st_old.astype(BF16)
        new_states = []
        y_parts = []
        for g in range(SSM_GROUPS):
            bm = xc[:, SSM_INNER + g * SSM_STATE:SSM_INNER + (g + 1) * SSM_STATE]
            cm_b = xc[:, SSM_INNER + gw + g * SSM_STATE:SSM_INNER + gw + (g + 1) * SSM_STATE].astype(BF16)
            cb = _dot_nt(cm_b, bm.astype(BF16))
            hpg = SSM_HEADS // SSM_GROUPS
            gl = slice(g * hpg * SSM_HEAD_DIM, (g + 1) * hpg * SSM_HEAD_DIM)
            y_off = _dot(cm_b, st_b[:, gl]) * ecs_e[:, gl]
            new_states.append(_dot(bm.T.astype(BF16), xdd_b[:, gl]))
            diag = []
            for pr in range(hpg // 2):
                xp = xd_b[:, gl][:, pr * LANES:(pr + 1) * LANES]
                res = []
                for hh in range(2):
                    hd = g * hpg + 2 * pr + hh
                    seg = cs[:, MISC_DT + hd:MISC_DT + hd + 1] - cs_t[MISC_DT + hd:MISC_DT + hd + 1, :]
                    decay = jnp.exp(jnp.where(tril, seg, -jnp.inf))
                    res.append(_dot((cb * decay).astype(BF16), xp))
                diag.append(jnp.where(left, res[0], res[1]))
            y_parts.append(jnp.concatenate(diag, axis=-1) + y_off)
        y = y + jnp.concatenate(y_parts, axis=-1)
        st_ref[...] = st_old * ecs_e[CHUNK - 1:CHUNK, :] + jnp.concatenate(new_states, axis=-1)

        yz = y * _silu(z_ref[0, r0:r0 + CHUNK, :].astype(F32))
        yn = yz * lax.rsqrt(jnp.mean(yz * yz, axis=-1, keepdims=True) + EPS) * gn_ref[...]
        o_ref[0, r0:r0 + CHUNK, :] = yn.astype(BF16)


def _cumsum_rows(tril_b, x):
    hi = x.astype(BF16)
    r1 = x - hi.astype(F32)
    mid = r1.astype(BF16)
    lo = (r1 - mid.astype(F32)).astype(BF16)
    return _dot(tril_b, hi) + _dot(tril_b, mid) + _dot(tril_b, lo)


def _ssd(xbc, z, misc, cw, cb, dtb, alog, dsk, gn, ts):
    bsz, s, _ = xbc.shape
    blk = lambda w: pl.BlockSpec((1, ts, w), lambda bi, j: (bi, j, 0))
    return pl.pallas_call(
        _ssd_kernel,
        grid=(bsz, s // ts),
        in_specs=[blk(SSM_XBC), blk(SSM_INNER), blk(LANES),
                  _const_spec((HALO, SSM_XBC)), _const_spec((1, SSM_XBC)),
                  _const_spec((1, LANES)), _const_spec((1, LANES)),
                  _const_spec((1, SSM_INNER)), _const_spec((1, SSM_INNER))],
        out_specs=blk(SSM_INNER),
        out_shape=jax.ShapeDtypeStruct((bsz, s, SSM_INNER), BF16),
        scratch_shapes=[pltpu.VMEM((ts + HALO, SSM_XBC), F32),
                        pltpu.VMEM((SSM_STATE, SSM_INNER), F32)],
        compiler_params=pltpu.CompilerParams(
            dimension_semantics=("arbitrary", "arbitrary"), vmem_limit_bytes=VMEM_LIMIT),
        name="ssd",
    )(xbc, z, misc, cw, cb, dtb, alog, dsk, gn)


FF_CHUNK = D_FF // 2


def _ffn_kernel(x_ref, u_ref, att_ref, y_ref, wo_ref, g_ref, wup_ref, cw_ref, cb_ref, wdn_ref,
                o_ref, ext_a, ext_g, carry_ref, gated_ref):
    tm = x_ref.shape[1]
    j = pl.program_id(1)

    @pl.when(j == 0)
    def _():
        carry_ref[...] = jnp.zeros(carry_ref.shape, F32)

    x1 = (x_ref[0]
          + _dot(u_ref[0], wo_ref[0:CONV_CH, :])
          + _dot(att_ref[0], wo_ref[CONV_CH:CONV_CH + ATT_W, :])
          + _dot(y_ref[0], wo_ref[CONV_CH + ATT_W:D_MIX, :]))
    hb = (x1 * lax.rsqrt(jnp.mean(x1 * x1, axis=-1, keepdims=True) + EPS) * g_ref[...]).astype(BF16)

    off = HALO - (FFN_CONV_K - 1)

    def conv_half(ext, c0):
        ext[0:HALO, :] = carry_ref[:, c0:c0 + FF_CHUNK]
        ext[HALO:, :] = _dot(hb, wup_ref[:, c0:c0 + FF_CHUNK])
        carry_ref[:, c0:c0 + FF_CHUNK] = ext[tm:tm + HALO, :]
        out = jnp.broadcast_to(cb_ref[:, c0:c0 + FF_CHUNK], (tm, FF_CHUNK))
        for tap in range(FFN_CONV_K):
            out = out + cw_ref[tap:tap + 1, c0:c0 + FF_CHUNK] * ext[off + tap:off + tap + tm, :]
        return out

    for c in range(D_FF // FF_CHUNK):
        fa = conv_half(ext_a, c * FF_CHUNK)
        fg = conv_half(ext_g, D_FF + c * FF_CHUNK)
        gated_ref[:, c * FF_CHUNK:(c + 1) * FF_CHUNK] = (_silu(fg) * fa).astype(BF16)
    o_ref[0] = x1 + _dot(gated_ref[...], wdn_ref[...])


def _ffn(x, u, att, y, wo, g, wup, cw, cb, wdn, tm):
    bsz, s, d = x.shape
    blk = lambda w: pl.BlockSpec((1, tm, w), lambda bi, j: (bi, j, 0))
    single = lambda shape: pl.BlockSpec(shape, lambda *_: (0,) * len(shape),
                                        pipeline_mode=pl.Buffered(1))
    return pl.pallas_call(
        _ffn_kernel,
        grid=(bsz, s // tm),
        in_specs=[blk(d), blk(CONV_CH), blk(ATT_W), blk(SSM_INNER),
                  single((D_MIX, d)), _const_spec((1, d)),
                  single((d, 2 * D_FF)), _const_spec((HALO, 2 * D_FF)), _const_spec((1, 2 * D_FF)),
                  single((D_FF, d))],
        out_specs=blk(d),
        out_shape=jax.ShapeDtypeStruct((bsz, s, d), F32),
        scratch_shapes=[pltpu.VMEM((tm + HALO, FF_CHUNK), F32),
                        pltpu.VMEM((tm + HALO, FF_CHUNK), F32),
                        pltpu.VMEM((HALO, 2 * D_FF), F32),
                        pltpu.VMEM((tm, D_FF), BF16)],
        input_output_aliases={0: 0},
        compiler_params=pltpu.CompilerParams(
            dimension_semantics=("arbitrary", "arbitrary"), vmem_limit_bytes=VMEM_LIMIT),
        name="ffn",
    )(x, u, att, y, wo, g, wup, cw, cb, wdn)


def _t5_bucket(rel):
    n = jnp.maximum(rel, 0)
    max_exact = NUM_BUCKETS // 2
    nf = jnp.maximum(n, 1).astype(F32)
    large = max_exact + (jnp.log(nf / max_exact) / math.log(MAX_DISTANCE / max_exact)
                         * (NUM_BUCKETS - max_exact)).astype(jnp.int32)
    large = jnp.minimum(large, NUM_BUCKETS - 1)
    return jnp.where(n < max_exact, n, large)


def _bias_tables(rel_bias):
    d = jnp.arange(2 * (QB // LANES))[:, None, None]
    kk = jnp.arange(LANES)[None, :, None]
    tq = jnp.arange(LANES)[None, None, :]
    rel = (d - QB // LANES + 1) * LANES + tq - kk
    onehot = (_t5_bucket(rel)[..., None] == jnp.arange(NUM_BUCKETS)).astype(F32)
    near = jnp.einsum("dkqb,bh->hdkq", onehot, rel_bias.astype(F32),
                      precision=lax.Precision.HIGHEST)
    far = rel_bias[_t5_bucket(jnp.int32(LANES + 1))].astype(F32)
    return near, far


def _pad_rows(w, rows):
    return jnp.concatenate([w, jnp.zeros((w.shape[0], rows - w.shape[1], w.shape[2]), w.dtype)], axis=1)


def _pad_lanes(v, start, width=LANES):
    out = jnp.zeros(v.shape[:-1] + (width,), v.dtype)
    return out.at[..., start:start + v.shape[-1]].set(v)


def kernel(x, rel_bias, norm_mix_g, w_in, conv_dw_w, conv_dw_b, conv_ln_g, conv_ln_b, q_norm_g, k_norm_g, ssm_conv_w, ssm_conv_b, dt_bias, a_log, d_skip, ssm_norm_g, w_out, norm_ffn_g, w_up, ffn_conv_w, ffn_conv_b, w_down):
    bsz, s, d = x.shape
    depth = w_in.shape[0]
    assert s % KC == 0 and s >= 2 * QB

    sizes = [CONV_CH, CONV_CH, ATT_W, ATT_W, ATT_W, N_IDX_HEADS * IDX_DIM, IDX_DIM, N_IDX_HEADS,
             SSM_INNER, SSM_XBC, SSM_HEADS]
    splits = [int(v) for v in np.cumsum(sizes)[:-1]]
    ca, cg, wq, wk, wv, wiq, wik, wiw, wz, wxbc, wdt = jnp.split(w_in, splits, axis=-1)
    w_main = jnp.concatenate([ca, cg, wq, wk, wiq, wik, wik, wz, wxbc], axis=-1).astype(BF16)
    w_vt = jnp.swapaxes(wv, 1, 2).astype(BF16)
    w_misc = jnp.concatenate(
        [_pad_lanes(wiw, 0, MISC_DT), _pad_lanes(wdt, 0, LANES - MISC_DT)], axis=-1).astype(BF16)
    seg = jnp.asarray(np.kron(np.eye(N_HEADS), np.full((HEAD_DIM, HEAD_DIM), 1.0 / HEAD_DIM)), BF16)

    near_bias, far_bias = _bias_tables(rel_bias)
    logit_bound = (BOUND_SLACK * HEAD_DIM ** 0.5
                   * jnp.max(jnp.abs(q_norm_g), axis=-1, keepdims=True)
                   * jnp.max(jnp.abs(k_norm_g), axis=-1, keepdims=True)
                   + jnp.max(rel_bias, axis=0)[None, :]).astype(F32)
    head_scalars = jnp.stack([jnp.broadcast_to(far_bias, logit_bound.shape), logit_bound], axis=1)

    layer_params = dict(
        g_mix=norm_mix_g[:, None, :], w_main=w_main, w_misc=w_misc, w_vt=w_vt,
        head_scalars=head_scalars,
        gq=jnp.tile(q_norm_g, (1, N_HEADS))[:, None, :], gk=jnp.tile(k_norm_g, (1, N_HEADS))[:, None, :],
        conv_w=_pad_rows(conv_dw_w, CONV_HALO), conv_b=conv_dw_b[:, None, :],
        ln_g=conv_ln_g[:, None, :], ln_b=conv_ln_b[:, None, :],
        ssm_w=_pad_rows(ssm_conv_w, HALO), ssm_b=ssm_conv_b[:, None, :],
        dtb=_pad_lanes(dt_bias, MISC_DT)[:, None, :], alog=_pad_lanes(a_log, MISC_DT)[:, None, :],
        dsk=jnp.repeat(d_skip, SSM_HEAD_DIM, axis=-1)[:, None, :], gn=ssm_norm_g[:, None, :],
        w_out=w_out.astype(BF16), g_ffn=norm_ffn_g[:, None, :], w_up=w_up.astype(BF16),
        ffn_w=_pad_rows(ffn_conv_w, HALO), ffn_b=ffn_conv_b[:, None, :], w_down=w_down.astype(BF16),
    )

    tm = min(512, s)

    def layer(xc, p):
        (cacg, q, k, vt, iq, ik2, z, xbc, misc) = _in_proj(
            xc.reshape(bsz * s, d), p["g_mix"], p["w_main"], p["w_misc"], p["w_vt"], p["gq"], p["gk"],
            seg, tm)
        r3 = lambda a: a.reshape(bsz, s, a.shape[-1])
        u = _conv_module(r3(cacg), p["conv_w"], p["conv_b"], p["ln_g"], p["ln_b"], tm)
        att = _dsa(p["head_scalars"], r3(q), r3(k), vt.reshape(bsz, s // LANES, ATT_W, LANES), r3(iq), r3(ik2),
                   r3(misc), near_bias)
        y = _ssd(r3(xbc), r3(z), r3(misc), p["ssm_w"], p["ssm_b"], p["dtb"], p["alog"],
                 p["dsk"], p["gn"], tm)
        xn = _ffn(xc, u, att, y, p["w_out"], p["g_ffn"], p["w_up"], p["ffn_w"], p["ffn_b"],
                  p["w_down"], tm)
        return xn, None

    out, _ = lax.scan(layer, x, layer_params)
    return out
```

```python
import functools
import math

import jax
import jax.numpy as jnp
import numpy as np
from jax import lax
from jax.experimental import pallas as pl
from jax.experimental.pallas import tpu as pltpu

F32 = jnp.float32
BF16 = jnp.bfloat16

EPS = 1e-6
LANES = 128
SUBLANES = 8
BF16_ROWS = 16
CONV_CH = 512
CONV_K = 31
N_HEADS = 8
HEAD_DIM = 64
ATT_W = N_HEADS * HEAD_DIM
N_IDX_HEADS = 4
IDX_DIM = 64
TOPK_MAX = 256
NUM_BUCKETS = 32
MAX_DISTANCE = 128
SSM_HEADS = 8
SSM_HEAD_DIM = 64
SSM_INNER = SSM_HEADS * SSM_HEAD_DIM
SSM_GROUPS = 2
SSM_STATE = 128
SSM_CONV_K = 4
SSM_XBC = SSM_INNER + 2 * SSM_GROUPS * SSM_STATE
CHUNK = 128
D_MIX = CONV_CH + ATT_W + SSM_INNER
D_FF = 2816
FFN_CONV_K = 3

MISC_IW = 0
MISC_DT = 8
HALO = 8
CONV_HALO = 32
QB = 512
KC = 512
GROUPS = 256
NEG = -1e30
DENOM_FLOOR = 1e-25
BOUND_SLACK = 1.03
VMEM_LIMIT = 56 * 1024 * 1024

NT_DIMS = (((1,), (1,)), ((), ()))


def _dot(a, b):
    return jnp.dot(a, b, preferred_element_type=F32)


def _dot_nt(a, b):
    return lax.dot_general(a, b, NT_DIMS, preferred_element_type=F32)


def _dot_split(x, m_bf16):
    hi = x.astype(BF16)
    r1 = x - hi.astype(F32)
    mid = r1.astype(BF16)
    lo = (r1 - mid.astype(F32)).astype(BF16)
    return _dot(hi, m_bf16) + _dot(mid, m_bf16) + _dot(lo, m_bf16)


def _silu(x):
    return x * jax.nn.sigmoid(x)


def _const_spec(shape):
    n = len(shape)
    return pl.BlockSpec(shape, lambda *_: (0,) * n)


def _in_proj_kernel(x_ref, g_ref, w_ref, wm_ref, wvt_ref, gq_ref, gk_ref, seg_ref,
                    cacg_ref, q_ref, k_ref, vt_ref, iq_ref, ik_ref, z_ref, xbc_ref, misc_ref):
    x = x_ref[...]
    h = x * lax.rsqrt(jnp.mean(x * x, axis=-1, keepdims=True) + EPS) * g_ref[...]
    hb = h.astype(BF16)

    def proj(c0, width):
        return _dot(hb, w_ref[:, c0:c0 + width])

    cacg_ref[...] = proj(0, 1024).astype(BF16)

    def qk_norm(c0, gain_ref, scale):
        t = proj(c0, ATT_W)
        ms = _dot((t * t).astype(BF16), seg_ref[...])
        return (t * lax.rsqrt(ms + EPS) * (gain_ref[...] * scale)).astype(BF16)

    q_ref[...] = qk_norm(1024, gq_ref, HEAD_DIM ** -0.5)
    k_ref[...] = qk_norm(1536, gk_ref, 1.0)
    vt = _dot_nt(wvt_ref[...], hb).astype(BF16)
    for j in range(vt_ref.shape[0]):
        vt_ref[j] = vt[:, j * LANES:(j + 1) * LANES]
    iq_ref[...] = proj(2048, N_IDX_HEADS * IDX_DIM).astype(BF16)
    ik_ref[...] = proj(2304, LANES).astype(BF16)
    z_ref[...] = proj(2432, SSM_INNER).astype(BF16)
    xbc_ref[...] = proj(2944, SSM_XBC).astype(BF16)
    misc_ref[...] = _dot(hb, wm_ref[...])


W_MAIN_COLS = 2944 + SSM_XBC


def _in_proj(x2, g, w_main, w_misc, w_vt, gq, gk, seg, tm):
    t, d = x2.shape
    rows = lambda w, dt: (pl.BlockSpec((tm, w), lambda i: (i, 0)), jax.ShapeDtypeStruct((t, w), dt))
    vt_out = (pl.BlockSpec((tm // LANES, ATT_W, LANES), lambda i: (i, 0, 0)),
              jax.ShapeDtypeStruct((t // LANES, ATT_W, LANES), BF16))
    outs = [rows(1024, BF16), rows(ATT_W, BF16), rows(ATT_W, BF16), vt_out,
            rows(N_IDX_HEADS * IDX_DIM, BF16), rows(LANES, BF16), rows(SSM_INNER, BF16),
            rows(SSM_XBC, BF16), rows(LANES, F32)]
    return pl.pallas_call(
        _in_proj_kernel,
        grid=(t // tm,),
        in_specs=[pl.BlockSpec((tm, d), lambda i: (i, 0)),
                  _const_spec((1, d)),
                  _const_spec((d, W_MAIN_COLS)),
                  _const_spec((d, LANES)),
                  _const_spec((ATT_W, d)),
                  _const_spec((1, ATT_W)), _const_spec((1, ATT_W)),
                  _const_spec((ATT_W, ATT_W))],
        out_specs=[spec for spec, _ in outs],
        out_shape=[shape for _, shape in outs],
        compiler_params=pltpu.CompilerParams(
            dimension_semantics=("arbitrary",), vmem_limit_bytes=VMEM_LIMIT),
        name="in_proj",
    )(x2, g, w_main, w_misc, w_vt, gq, gk, seg)


CONV_ROWS = 64


def _conv_kernel(cacg_ref, w_ref, b_ref, g_ref, beta_ref, o_ref, ext_ref):
    ts = o_ref.shape[1]
    j = pl.program_id(1)

    @pl.when(j == 0)
    def _():
        ext_ref[0:CONV_HALO, :] = jnp.zeros((CONV_HALO, CONV_CH), F32)

    @pl.when(j > 0)
    def _():
        ext_ref[0:CONV_HALO, :] = ext_ref[ts:ts + CONV_HALO, :]

    ca = cacg_ref[0, :, 0:CONV_CH].astype(F32)
    cg = cacg_ref[0, :, CONV_CH:2 * CONV_CH].astype(F32)
    ext_ref[CONV_HALO:, :] = ca * jax.nn.sigmoid(cg)

    w = w_ref[...]
    off = CONV_HALO - (CONV_K - 1)
    for r in range(ts // CONV_ROWS):
        base = r * CONV_ROWS
        acc = jnp.broadcast_to(b_ref[...], (CONV_ROWS, CONV_CH))
        for res in range(SUBLANES):
            rows = CONV_ROWS + (SUBLANES if res else 0)
            part = None
            for tap in range(CONV_K):
                if (off + tap) % SUBLANES != res:
                    continue
                start = base + off + tap - res
                term = w[tap:tap + 1, :] * ext_ref[start:start + rows, :]
                part = term if part is None else part + term
            acc = acc + part[res:res + CONV_ROWS, :]
        mu = jnp.mean(acc, axis=-1, keepdims=True)
        cen = acc - mu
        var = jnp.mean(cen * cen, axis=-1, keepdims=True)
        yn = cen * lax.rsqrt(var + EPS) * g_ref[...] + beta_ref[...]
        o_ref[0, base:base + CONV_ROWS, :] = _silu(yn).astype(BF16)


def _conv_module(cacg, w, b, g, beta, ts):
    bsz, s, _ = cacg.shape
    return pl.pallas_call(
        _conv_kernel,
        grid=(bsz, s // ts),
        in_specs=[pl.BlockSpec((1, ts, 2 * CONV_CH), lambda bi, j: (bi, j, 0)),
                  _const_spec((CONV_HALO, CONV_CH)),
                  _const_spec((1, CONV_CH)), _const_spec((1, CONV_CH)), _const_spec((1, CONV_CH))],
        out_specs=pl.BlockSpec((1, ts, CONV_CH), lambda bi, j: (bi, j, 0)),
        out_shape=jax.ShapeDtypeStruct((bsz, s, CONV_CH), BF16),
        scratch_shapes=[pltpu.VMEM((ts + CONV_HALO, CONV_CH), F32)],
        compiler_params=pltpu.CompilerParams(
            dimension_semantics=("arbitrary", "arbitrary"), vmem_limit_bytes=VMEM_LIMIT),
        name="conv_module",
    )(cacg, w, b, g, beta)


def _key_to_float(key):
    bits = key ^ ((key >> 31) & jnp.int32(0x7FFFFFFF))
    return lax.bitcast_convert_type(bits, F32)


def _float_to_key(v):
    bits = lax.bitcast_convert_type(v, jnp.int32)
    return bits ^ ((bits >> 31) & jnp.int32(0x7FFFFFFF))


KEY_NEG_INF = int(np.int32(np.uint32(0xFF800000)) ^ np.int32(0x7FFFFFFF))


def _dsa_kernel(hb_ref, q_ref, k_ref, vt_ref, iq_ref, ik_ref, misc_ref, nb_ref, o_ref,
                sc_ref, qm_ref, acc_ref, thr_ref, l_ref, *, topk):
    i = pl.program_id(1)
    s_len = k_ref.shape[1]
    q0 = i * QB
    lane = lax.broadcasted_iota(jnp.int32, (LANES, QB), 1)
    row = lax.broadcasted_iota(jnp.int32, (LANES, QB), 0)
    left = lax.broadcasted_iota(jnp.int32, (QB, LANES), 1) < HEAD_DIM
    qpos = q0 + lane

    iq = iq_ref[0]
    misc_t = misc_ref[0].T
    iqm, iw = [], []
    for h in range(N_IDX_HEADS):
        pair = iq[:, (h // 2) * LANES:(h // 2 + 1) * LANES]
        iqm.append(jnp.where(left if h % 2 == 0 else ~left, pair, jnp.zeros_like(pair)))
        iw.append(misc_t[MISC_IW + h:MISC_IW + h + 1, :] * (IDX_DIM ** -0.5 * N_IDX_HEADS ** -0.5))

    def fold(x, op):
        return op(x.reshape(LANES // SUBLANES, SUBLANES, QB), axis=0)

    def score_chunk(c, carry):
        gmax, n_ge0, n_gt0 = carry
        gmax = list(gmax)
        k0 = pl.multiple_of(c * KC, KC)
        ikc = ik_ref[0, pl.ds(k0, KC), :]
        s = jnp.zeros((KC, QB), F32)
        for h in range(N_IDX_HEADS):
            s = s + jnp.maximum(_dot_nt(ikc, iqm[h]), 0.0) * iw[h]
        for t in range(KC // LANES):
            kpos = k0 + t * LANES + row
            blk = jnp.where(kpos <= qpos, s[t * LANES:(t + 1) * LANES, :], -jnp.inf)
            sc_ref[c * (KC // LANES) + t] = blk
            g = t % (GROUPS // LANES)
            gmax[g] = jnp.maximum(gmax[g], blk)
            n_ge0 = n_ge0 + fold(jnp.where(blk >= 0.0, 1.0, 0.0), jnp.sum)
            n_gt0 = n_gt0 + fold(jnp.where(blk > 0.0, 1.0, 0.0), jnp.sum)
        return tuple(gmax), n_ge0, n_gt0

    per_chunk = KC // LANES
    n_chunks = (q0 + QB - 1) // KC + 1
    gmax, n_ge0, n_gt0 = lax.fori_loop(
        0, n_chunks, score_chunk,
        (tuple(jnp.full((LANES, QB), -jnp.inf, F32) for _ in range(GROUPS // LANES)),
         jnp.zeros((SUBLANES, QB), F32), jnp.zeros((SUBLANES, QB), F32)))

    def count_blocks(pred):
        def body(c, acc):
            for t in range(per_chunk):
                bk = c * per_chunk + t
                acc = acc + fold(jnp.where(pred(sc_ref[bk], bk), 1.0, 0.0), jnp.sum)
            return acc
        acc = lax.fori_loop(0, n_chunks, body, jnp.zeros((SUBLANES, QB), F32))
        return jnp.sum(acc, axis=0, keepdims=True)

    everything = jnp.float32(2 * s_len)
    vmax = jnp.max(functools.reduce(jnp.maximum, gmax), axis=0, keepdims=True)
    glow = jnp.min(functools.reduce(jnp.minimum, gmax), axis=0, keepdims=True)
    at0 = jnp.sum(n_ge0, axis=0, keepdims=True)
    above0 = jnp.sum(n_gt0, axis=0, keepdims=True)
    key_glow = _float_to_key(glow)
    positive = above0 >= topk
    negative = at0 < topk
    lo0 = jnp.where(positive, jnp.maximum(key_glow, 1), key_glow)
    c_lo0 = jnp.where(positive & (key_glow < 1), above0, everything)
    hi0 = jnp.where(negative, 0, _float_to_key(vmax) + 1)
    c_hi0 = jnp.where(negative, at0, 0.0)
    zero_tied = jnp.logical_not(positive | negative)
    closed0 = lo0 + 1 >= hi0
    done0 = zero_tied | closed0
    thr0 = jnp.where(zero_tied, 0, lo0)
    n_ge_init = jnp.where(zero_tied, at0, c_lo0)
    need0 = jnp.where(zero_tied, topk - above0, topk - c_hi0)

    steps_per_check = 4

    def search_cond(state):
        it, _, _, _, _, done, _, _, _ = state
        return (it < 36) & (jnp.min(done) == 0)

    def search_steps(state):
        return lax.fori_loop(0, steps_per_check, lambda _, st: search_step(st), state)

    def search_step(state):
        it, lo, hi, c_lo, c_hi, done, thr_key, n_ge, need = state
        mid = (lo >> 1) + (hi >> 1) + (lo & hi & 1)
        mid_b = jnp.broadcast_to(_key_to_float(mid), (LANES, QB))
        cnt = count_blocks(lambda s, bk: s >= mid_b)
        ok = cnt >= topk
        lo2, c_lo2 = jnp.where(ok, mid, lo), jnp.where(ok, cnt, c_lo)
        hi2, c_hi2 = jnp.where(ok, hi, mid), jnp.where(ok, c_hi, cnt)
        hit = cnt == topk
        newly = (done == 0) & (hit | (lo2 + 1 == hi2))
        thr_key = jnp.where(newly, jnp.where(hit, mid, lo2), thr_key)
        n_ge = jnp.where(newly, jnp.where(hit, float(topk), c_lo2), n_ge)
        need = jnp.where(newly, jnp.where(hit, everything, topk - c_hi2), need)
        frozen = done == 1
        return (it + 1, jnp.where(frozen, lo, lo2), jnp.where(frozen, hi, hi2),
                jnp.where(frozen, c_lo, c_lo2), jnp.where(frozen, c_hi, c_hi2),
                jnp.where(newly, 1, done), thr_key, n_ge, need)

    state = lax.while_loop(
        search_cond, search_steps,
        (jnp.int32(0), lo0, hi0, c_lo0, c_hi0, done0.astype(jnp.int32), thr0, n_ge_init, need0))
    thr = _key_to_float(state[6])
    n_ge, need = state[7], state[8]
    thr_b = jnp.broadcast_to(thr, (LANES, QB))
    thr_ref[...] = thr_b

    @pl.when(jnp.max(n_ge) > topk)
    def _():
        rank_mat = jnp.where(
            lax.broadcasted_iota(jnp.int32, (LANES, LANES), 0)
            >= lax.broadcasted_iota(jnp.int32, (LANES, LANES), 1), 1.0, 0.0).astype(BF16)

        def demote(c, seen):
            for t in range(per_chunk):
                bk = c * per_chunk + t
                s = sc_ref[bk]
                tie = s == thr_b
                ind = jnp.where(tie, 1.0, 0.0)
                rank = seen + _dot(rank_mat, ind.astype(BF16))
                sc_ref[bk] = jnp.where(tie & (rank > need), -jnp.inf, s)
                seen = seen + jnp.sum(ind, axis=0, keepdims=True)
            return seen

        lax.fori_loop(0, n_chunks, demote, jnp.zeros((1, QB), F32))

    q = q_ref[0]
    for pr in range(N_HEADS // 2):
        pair = q[:, pr * LANES:(pr + 1) * LANES]
        zero = jnp.zeros_like(pair)
        qm_ref[pr] = jnp.concatenate([jnp.where(left, pair, zero), jnp.where(left, zero, pair)], axis=0)
    odd =lax.broadcasted_iota(jnp.int32, (1, 2 * QB), 1) >= QB

    def mask_tile(bk, active, causal):
        t = jnp.where(active, thr_ref[...], jnp.inf)
        sel = sc_ref[bk] >= t
        if causal:
            sel = sel & (bk * LANES + row <= qpos)
        return jnp.where(sel, 0.0, NEG)

    def pair_row(values, pr):
        return jnp.where(odd, values[2 * pr + 1], values[2 * pr])

    def attend(blocks, maskadd, near, state, robust):
        width = len(blocks) * LANES
        k0 = pl.multiple_of(blocks[0] * LANES, LANES)
        maskadd2 = jnp.concatenate([maskadd, maskadd], axis=1)
        far = [hb_ref[0, h] for h in range(N_HEADS)]
        bound = [hb_ref[1, h] for h in range(N_HEADS)]
        new_state = []
        for pr in range(N_HEADS // 2):
            kp = k_ref[0, pl.ds(k0, width), pr * LANES:(pr + 1) * LANES]
            lg = _dot_nt(kp, qm_ref[pr]) + maskadd2
            if near is None:
                bias = pair_row(far, pr)
            else:
                n_w = len(blocks)
                lg = lg + jnp.concatenate(
                    [jnp.concatenate(
                        [nb_ref[2 * pr + hh, jnp.maximum(qt - w - near + n_w - 1, 0)]
                         for hh in range(2) for qt in range(QB // LANES)], axis=1)
                     for w in range(n_w)], axis=0)
                bias = 0.0
            if robust:
                m_old, l_old = state[pr]
                m_new = jnp.maximum(m_old, jnp.max(lg, axis=0, keepdims=True) + bias)
                alpha = jnp.exp(m_old - m_new)
                p = jnp.exp(lg - (m_new - bias))
            else:
                alpha = None
                p = jnp.exp(lg + (bias - pair_row(bound, pr)))
            rows = slice(pr * LANES, (pr + 1) * LANES)
            vt = jnp.concatenate([vt_ref[0, bk, rows, :] for bk in blocks], axis=1)
            lhs = jnp.concatenate([vt, jnp.ones((BF16_ROWS, width), BF16)], axis=0)
            pv = _dot(lhs, p.astype(BF16))
            colsum = pv[LANES:LANES + 1, :]
            new_state.append((m_new, alpha * l_old + colsum) if robust else state[pr] + colsum)
            for hh in range(2):
                hs = slice(pr * LANES + hh * HEAD_DIM, pr * LANES + (hh + 1) * HEAD_DIM)
                cs = slice(hh * QB, (hh + 1) * QB)
                upd = pv[hh * HEAD_DIM:(hh + 1) * HEAD_DIM, cs]
                acc_ref[hs, :] = acc_ref[hs, :] + upd if alpha is None else alpha[:, cs] * acc_ref[hs, :] + upd
        return new_state

    far_end = jnp.maximum(q0 - LANES, 0)
    window = [far_end // LANES + t for t in range(1 + QB // LANES)]
    window_shift = jnp.where(i == 0, 1, 0)

    def attention(robust):
        acc_ref[...] = jnp.zeros(acc_ref.shape, F32)
        zero = jnp.zeros((1, 2 * QB), F32)
        state = [(jnp.full((1, 2 * QB), NEG, F32), zero) if robust else zero
                 for _ in range(N_HEADS // 2)]

        def far_chunk(c, state):
            blocks = [c * per_chunk + t for t in range(per_chunk)]
            maskadd = jnp.concatenate(
                [mask_tile(bk, bk * LANES < far_end, False) for bk in blocks], axis=0)
            return attend(blocks, maskadd, None, state, robust)

        state = lax.fori_loop(0, (far_end + KC - 1) // KC, far_chunk, state)
        state = attend(window, jnp.concatenate([mask_tile(bk, True, True) for bk in window], axis=0),
                       window_shift, state, robust)
        for pr in range(N_HEADS // 2):
            l_ref[pr:pr + 1, :] = state[pr][1] if robust else state[pr]

    attention(robust=False)

    @pl.when(jnp.logical_not(jnp.min(l_ref[...]) > DENOM_FLOOR))
    def _():
        attention(robust=True)

    out_t = jnp.concatenate(
        [acc_ref[h * HEAD_DIM:(h + 1) * HEAD_DIM, :]
         / l_ref[h // 2:h // 2 + 1, (h % 2) * QB:(h % 2 + 1) * QB] for h in range(N_HEADS)],
        axis=0)
    o_ref[0] = out_t.T.astype(BF16)


def _dsa(head_scalars, q, k, vt, iq, ik2, misc, near_bias):
    bsz, s, _ = q.shape
    topk = min(TOPK_MAX, s // 4)
    nq = s // QB
    blk = lambda w: pl.BlockSpec((1, QB, w), lambda bi, i: (bi, i, 0))
    once = pl.Buffered(1)
    full = lambda w: pl.BlockSpec((1, s, w), lambda bi, i: (bi, 0, 0), pipeline_mode=once)
    return pl.pallas_call(
        functools.partial(_dsa_kernel, topk=topk),
        grid=(bsz, nq),
        in_specs=[pl.BlockSpec(memory_space=pltpu.SMEM),
                  blk(ATT_W), full(ATT_W),
                  pl.BlockSpec((1, s // LANES, ATT_W, LANES), lambda bi, i: (bi, 0, 0, 0),
                               pipeline_mode=once),
                  blk(N_IDX_HEADS * IDX_DIM), full(LANES), blk(LANES),
                  pl.BlockSpec((N_HEADS, 2 * (QB // LANES), LANES, LANES), lambda bi, i: (0, 0, 0, 0),
                               pipeline_mode=once)],
        out_specs=blk(ATT_W),
        out_shape=jax.ShapeDtypeStruct((bsz, s, ATT_W), BF16),
        scratch_shapes=[
            pltpu.VMEM((_round_up(s, KC) // LANES, LANES, QB), F32),
            pltpu.VMEM((N_HEADS // 2, 2 * QB, LANES), BF16),
            pltpu.VMEM((ATT_W, QB), F32),
            pltpu.VMEM((LANES, QB), F32),
            pltpu.VMEM((N_HEADS // 2, 2 * QB), F32),
        ],
        compiler_params=pltpu.CompilerParams(
            dimension_semantics=("arbitrary", "arbitrary"), vmem_limit_bytes=VMEM_LIMIT),
        name="dsa",
    )(head_scalars, q, k, vt, iq, ik2, misc, near_bias)


def _round_up(a, b):
    return (a + b - 1) // b * b


def _ssd_kernel(xbc_ref, z_ref, misc_ref, cw_ref, cb_ref, dtb_ref, alog_ref, dsk_ref, gn_ref,
                o_ref, ext_ref, st_ref):
    ts = o_ref.shape[1]
    j = pl.program_id(1)

    @pl.when(j == 0)
    def _():
        ext_ref[0:HALO, :] = jnp.zeros((HALO, SSM_XBC), F32)
        st_ref[...] = jnp.zeros(st_ref.shape, F32)

    @pl.when(j > 0)
    def _():
        ext_ref[0:HALO, :] = ext_ref[ts:ts + HALO, :]

    ext_ref[HALO:, :] = xbc_ref[0].astype(F32)

    lane = lax.broadcasted_iota(jnp.int32, (CHUNK, LANES), 1)
    row = lax.broadcasted_iota(jnp.int32, (CHUNK, LANES), 0)
    tril = row >= lane
    tril_b = jnp.where(tril, 1.0, 0.0).astype(BF16)
    is_dt = (lane >= MISC_DT) & (lane < MISC_DT + SSM_HEADS)
    e_row = lax.broadcasted_iota(jnp.int32, (LANES, SSM_INNER), 0)
    e_col = lax.broadcasted_iota(jnp.int32, (LANES, SSM_INNER), 1)
    expand = jnp.where(e_row - MISC_DT == e_col // SSM_HEAD_DIM, 1.0, 0.0).astype(BF16)
    left = lane < SSM_HEAD_DIM
    lane1 = lax.broadcasted_iota(jnp.int32, (1, LANES), 1)
    a_row = jnp.where((lane1 >= MISC_DT) & (lane1 < MISC_DT + SSM_HEADS), -jnp.exp(alog_ref[...]), 0.0)
    cw = cw_ref[...]
    off = HALO - (SSM_CONV_K - 1)
    gw = SSM_GROUPS * SSM_STATE

    for c in range(ts // CHUNK):
        r0 = c * CHUNK
        conv = jnp.broadcast_to(cb_ref[...], (CHUNK, SSM_XBC))
        for tap in range(SSM_CONV_K):
            conv = conv + cw[tap:tap + 1, :] * ext_ref[r0 + off + tap:r0 + off + tap + CHUNK, :]
        xc = _silu(conv)
        xs = xc[:, 0:SSM_INNER]

        dt = jnp.where(is_dt, jax.nn.softplus(misc_ref[0, r0:r0 + CHUNK, :] + dtb_ref[...]), 0.0)
        cs = _cumsum_rows(tril_b, dt * a_row)
        cs_last = cs[CHUNK - 1:CHUNK, :]
        dt_e = _dot_split(dt, expand)
        dec_e = _dot_split(jnp.exp(cs_last - cs), expand)
        ecs_e = _dot_split(jnp.exp(cs), expand)
        cs_t = cs.T

        xd = xs * dt_e
        xd_b = xd.astype(BF16)
        xdd_b = (xd * dec_e).astype(BF16)
        y = dsk_ref[...] * xs
        st_old = st_ref[...]
        st_b = st_old.astype(BF16)
        new_states = []
        y_parts = []
        for g in range(SSM_GROUPS):
            bm = xc[:, SSM_INNER + g * SSM_STATE:SSM_INNER + (g + 1) * SSM_STATE]
            cm_b = xc[:, SSM_INNER + gw + g * SSM_STATE:SSM_INNER + gw + (g + 1) * SSM_STATE].astype(BF16)
            cb = _dot_nt(cm_b, bm.astype(BF16))
            hpg = SSM_HEADS // SSM_GROUPS
            gl = slice(g * hpg * SSM_HEAD_DIM, (g + 1) * hpg * SSM_HEAD_DIM)
            y_off = _dot(cm_b, st_b[:, gl]) * ecs_e[:, gl]
            new_states.append(_dot(bm.T.astype(BF16), xdd_b[:, gl]))
            diag = []
            for pr in range(hpg // 2):
                xp = xd_b[:, gl][:, pr * LANES:(pr + 1) * LANES]
                res = []
                for hh in range(2):
                    hd = g * hpg + 2 * pr + hh
                    seg = cs[:, MISC_DT + hd:MISC_DT + hd + 1] - cs_t[MISC_DT + hd:MISC_DT + hd + 1, :]
                    decay = jnp.exp(jnp.where(tril, seg, -jnp.inf))
                    res.append(_dot((cb * decay).astype(BF16), xp))
                diag.append(jnp.where(left, res[0], res[1]))
            y_parts.append(jnp.concatenate(diag, axis=-1) + y_off)
        y = y + jnp.concatenate(y_parts, axis=-1)
        st_ref[...] = st_old * ecs_e[CHUNK - 1:CHUNK, :] + jnp.concatenate(new_states, axis=-1)

        yz = y * _silu(z_ref[0, r0:r0 + CHUNK, :].astype(F32))
        yn = yz * lax.rsqrt(jnp.mean(yz * yz, axis=-1, keepdims=True) + EPS) * gn_ref[...]
        o_ref[0, r0:r0 + CHUNK, :] = yn.astype(BF16)


def _cumsum_rows(tril_b, x):
    hi = x.astype(BF16)
    r1 = x - hi.astype(F32)
    mid = r1.astype(BF16)
    lo = (r1 - mid.astype(F32)).astype(BF16)
    return _dot(tril_b, hi) + _dot(tril_b, mid) + _dot(tril_b, lo)


def _ssd(xbc, z, misc, cw, cb, dtb, alog, dsk, gn, ts):
    bsz, s, _ = xbc.shape
    blk = lambda w: pl.BlockSpec((1, ts, w), lambda bi, j: (bi, j, 0))
    return pl.pallas_call(
        _ssd_kernel,
        grid=(bsz, s // ts),
        in_specs=[blk(SSM_XBC), blk(SSM_INNER), blk(LANES),
                  _const_spec((HALO, SSM_XBC)), _const_spec((1, SSM_XBC)),
                  _const_spec((1, LANES)), _const_spec((1, LANES)),
                  _const_spec((1, SSM_INNER)), _const_spec((1, SSM_INNER))],
        out_specs=blk(SSM_INNER),
        out_shape=jax.ShapeDtypeStruct((bsz, s, SSM_INNER), BF16),
        scratch_shapes=[pltpu.VMEM((ts + HALO, SSM_XBC), F32),
                        pltpu.VMEM((SSM_STATE, SSM_INNER), F32)],
        compiler_params=pltpu.CompilerParams(
            dimension_semantics=("arbitrary", "arbitrary"), vmem_limit_bytes=VMEM_LIMIT),
        name="ssd",
    )(xbc, z, misc, cw, cb, dtb, alog, dsk, gn)


FF_CHUNK = D_FF // 2


def _ffn_kernel(x_ref, u_ref, att_ref, y_ref, wo_ref, g_ref, wup_ref, cw_ref, cb_ref, wdn_ref,
                o_ref, ext_a, ext_g, carry_ref, gated_ref):
    tm = x_ref.shape[1]
    j = pl.program_id(1)

    @pl.when(j == 0)
    def _():
        carry_ref[...] = jnp.zeros(carry_ref.shape, F32)

    x1 = (x_ref[0]
          + _dot(u_ref[0], wo_ref[0:CONV_CH, :])
          + _dot(att_ref[0], wo_ref[CONV_CH:CONV_CH + ATT_W, :])
          + _dot(y_ref[0], wo_ref[CONV_CH + ATT_W:D_MIX, :]))
    hb = (x1 * lax.rsqrt(jnp.mean(x1 * x1, axis=-1, keepdims=True) + EPS) * g_ref[...]).astype(BF16)

    off = HALO - (FFN_CONV_K - 1)

    def conv_half(ext, c0):
        ext[0:HALO, :] = carry_ref[:, c0:c0 + FF_CHUNK]
        ext[HALO:, :] = _dot(hb, wup_ref[:, c0:c0 + FF_CHUNK])
        carry_ref[:, c0:c0 + FF_CHUNK] = ext[tm:tm + HALO, :]
        out = jnp.broadcast_to(cb_ref[:, c0:c0 + FF_CHUNK], (tm, FF_CHUNK))
        for tap in range(FFN_CONV_K):
            out = out + cw_ref[tap:tap + 1, c0:c0 + FF_CHUNK] * ext[off + tap:off + tap + tm, :]
        return out

    for c in range(D_FF // FF_CHUNK):
        fa = conv_half(ext_a, c * FF_CHUNK)
        fg = conv_half(ext_g, D_FF + c * FF_CHUNK)
        gated_ref[:, c * FF_CHUNK:(c + 1) * FF_CHUNK] = (_silu(fg) * fa).astype(BF16)
    o_ref[0] = x1 + _dot(gated_ref[...], wdn_ref[...])


def _ffn(x, u, att, y, wo, g, wup, cw, cb, wdn, tm):
    bsz, s, d = x.shape
    blk = lambda w: pl.BlockSpec((1, tm, w), lambda bi, j: (bi, j, 0))
    single = lambda shape: pl.BlockSpec(shape, lambda *_: (0,) * len(shape),
                                        pipeline_mode=pl.Buffered(1))
    return pl.pallas_call(
        _ffn_kernel,
        grid=(bsz, s // tm),
        in_specs=[blk(d), blk(CONV_CH), blk(ATT_W), blk(SSM_INNER),
                  single((D_MIX, d)), _const_spec((1, d)),
                  single((d, 2 * D_FF)), _const_spec((HALO, 2 * D_FF)), _const_spec((1, 2 * D_FF)),
                  single((D_FF, d))],
        out_specs=blk(d),
        out_shape=jax.ShapeDtypeStruct((bsz, s, d), F32),
        scratch_shapes=[pltpu.VMEM((tm + HALO, FF_CHUNK), F32),
                        pltpu.VMEM((tm + HALO, FF_CHUNK), F32),
                        pltpu.VMEM((HALO, 2 * D_FF), F32),
                        pltpu.VMEM((tm, D_FF), BF16)],
        input_output_aliases={0: 0},
        compiler_params=pltpu.CompilerParams(
            dimension_semantics=("arbitrary", "arbitrary"), vmem_limit_bytes=VMEM_LIMIT),
        name="ffn",
    )(x, u, att, y, wo, g, wup, cw, cb, wdn)


def _t5_bucket(rel):
    n = jnp.maximum(rel, 0)
    max_exact = NUM_BUCKETS // 2
    nf = jnp.maximum(n, 1).astype(F32)
    large = max_exact + (jnp.log(nf / max_exact) / math.log(MAX_DISTANCE / max_exact)
                         * (NUM_BUCKETS - max_exact)).astype(jnp.int32)
    large = jnp.minimum(large, NUM_BUCKETS - 1)
    return jnp.where(n < max_exact, n, large)


def _bias_tables(rel_bias):
    d = jnp.arange(2 * (QB // LANES))[:, None, None]
    kk = jnp.arange(LANES)[None, :, None]
    tq = jnp.arange(LANES)[None, None, :]
    rel = (d - QB // LANES + 1) * LANES + tq - kk
    onehot = (_t5_bucket(rel)[..., None] == jnp.arange(NUM_BUCKETS)).astype(F32)
    near = jnp.einsum("dkqb,bh->hdkq", onehot, rel_bias.astype(F32),
                      precision=lax.Precision.HIGHEST)
    far = rel_bias[_t5_bucket(jnp.int32(LANES + 1))].astype(F32)
    return near, far


def _pad_rows(w, rows):
    return jnp.concatenate([w, jnp.zeros((w.shape[0], rows - w.shape[1], w.shape[2]), w.dtype)], axis=1)


def _pad_lanes(v, start, width=LANES):
    out = jnp.zeros(v.shape[:-1] + (width,), v.dtype)
    return out.at[..., start:start + v.shape[-1]].set(v)


def kernel(x, rel_bias, norm_mix_g, w_in, conv_dw_w, conv_dw_b, conv_ln_g, conv_ln_b, q_norm_g, k_norm_g, ssm_conv_w, ssm_conv_b, dt_bias, a_log, d_skip, ssm_norm_g, w_out, norm_ffn_g, w_up, ffn_conv_w, ffn_conv_b, w_down):
    bsz, s, d = x.shape
    depth = w_in.shape[0]
    assert s % KC == 0 and s >= 2 * QB

    sizes = [CONV_CH, CONV_CH, ATT_W, ATT_W, ATT_W, N_IDX_HEADS * IDX_DIM, IDX_DIM, N_IDX_HEADS,
             SSM_INNER, SSM_XBC, SSM_HEADS]
    splits = [int(v) for v in np.cumsum(sizes)[:-1]]
    ca, cg, wq, wk, wv, wiq, wik, wiw, wz, wxbc, wdt = jnp.split(w_in, splits, axis=-1)
    w_main = jnp.concatenate([ca, cg, wq, wk, wiq, wik, wik, wz, wxbc], axis=-1).astype(BF16)
    w_vt = jnp.swapaxes(wv, 1, 2).astype(BF16)
    w_misc = jnp.concatenate(
        [_pad_lanes(wiw, 0, MISC_DT), _pad_lanes(wdt, 0, LANES - MISC_DT)], axis=-1).astype(BF16)
    seg = jnp.asarray(np.kron(np.eye(N_HEADS), np.full((HEAD_DIM, HEAD_DIM), 1.0 / HEAD_DIM)), BF16)

    near_bias, far_bias = _bias_tables(rel_bias)
    logit_bound = (BOUND_SLACK * HEAD_DIM ** 0.5
                   * jnp.max(jnp.abs(q_norm_g), axis=-1, keepdims=True)
                   * jnp.max(jnp.abs(k_norm_g), axis=-1, keepdims=True)
                   + jnp.max(rel_bias, axis=0)[None, :]).astype(F32)
    head_scalars = jnp.stack([jnp.broadcast_to(far_bias, logit_bound.shape), logit_bound], axis=1)

    layer_params = dict(
        g_mix=norm_mix_g[:, None, :], w_main=w_main, w_misc=w_misc, w_vt=w_vt,
        head_scalars=head_scalars,
        gq=jnp.tile(q_norm_g, (1, N_HEADS))[:, None, :], gk=jnp.tile(k_norm_g, (1, N_HEADS))[:, None, :],
        conv_w=_pad_rows(conv_dw_w, CONV_HALO), conv_b=conv_dw_b[:, None, :],
        ln_g=conv_ln_g[:, None, :], ln_b=conv_ln_b[:, None, :],
        ssm_w=_pad_rows(ssm_conv_w, HALO), ssm_b=ssm_conv_b[:, None, :],
        dtb=_pad_lanes(dt_bias, MISC_DT)[:, None, :], alog=_pad_lanes(a_log, MISC_DT)[:, None, :],
        dsk=jnp.repeat(d_skip, SSM_HEAD_DIM, axis=-1)[:, None, :], gn=ssm_norm_g[:, None, :],
        w_out=w_out.astype(BF16), g_ffn=norm_ffn_g[:, None, :], w_up=w_up.astype(BF16),
        ffn_w=_pad_rows(ffn_conv_w, HALO), ffn_b=ffn_conv_b[:, None, :], w_down=w_down.astype(BF16),
    )

    tm = min(512, s)

    def layer(xc, p):
        (cacg, q, k, vt, iq, ik2, z, xbc, misc) = _in_proj(
            xc.reshape(bsz * s, d), p["g_mix"], p["w_main"], p["w_misc"], p["w_vt"], p["gq"], p["gk"],
            seg, tm)
        r3 = lambda a: a.reshape(bsz, s, a.shape[-1])
        u = _conv_module(r3(cacg), p["conv_w"], p["conv_b"], p["ln_g"], p["ln_b"], tm)
        att = _dsa(p["head_scalars"], r3(q), r3(k), vt.reshape(bsz, s // LANES, ATT_W, LANES), r3(iq), r3(ik2),
                   r3(misc), near_bias)
        y = _ssd(r3(xbc), r3(z), r3(misc), p["ssm_w"], p["ssm_b"], p["dtb"], p["alog"],
                 p["dsk"], p["gn"], tm)
        xn = _ffn(xc, u, att, y, p["w_out"], p["g_ffn"], p["w_up"], p["ffn_w"], p["ffn_b"],
                  p["w_down"], tm)
        return xn, None

    out, _ = lax.scan(layer, x, layer_params)
    return out
```

```python
import functools
import math

import jax
import jax.numpy as jnp
import numpy as np
from jax import lax
from jax.experimental import pallas as pl
from jax.experimental.pallas import tpu as pltpu

F32 = jnp.float32
BF16 = jnp.bfloat16

EPS = 1e-6
LANES = 128
SUBLANES = 8
BF16_ROWS = 16
CONV_CH = 512
CONV_K = 31
N_HEADS = 8
HEAD_DIM = 64
ATT_W = N_HEADS * HEAD_DIM
N_IDX_HEADS = 4
IDX_DIM = 64
TOPK_MAX = 256
NUM_BUCKETS = 32
MAX_DISTANCE = 128
SSM_HEADS = 8
SSM_HEAD_DIM = 64
SSM_INNER = SSM_HEADS * SSM_HEAD_DIM
SSM_GROUPS = 2
SSM_STATE = 128
SSM_CONV_K = 4
SSM_XBC = SSM_INNER + 2 * SSM_GROUPS * SSM_STATE
CHUNK = 128
D_MIX = CONV_CH + ATT_W + SSM_INNER
D_FF = 2816
FFN_CONV_K = 3

MISC_IW = 0
MISC_DT = 8
HALO = 8
CONV_HALO = 32
QB = 512
KC = 512
GROUPS = 256
NEG = -1e30
DENOM_FLOOR = 1e-25
BOUND_SLACK = 1.03
VMEM_LIMIT = 56 * 1024 * 1024

NT_DIMS = (((1,), (1,)), ((), ()))


def _dot(a, b):
    return jnp.dot(a, b, preferred_element_type=F32)


def _dot_nt(a, b):
    return lax.dot_general(a, b, NT_DIMS, preferred_element_type=F32)


def _dot_split(x, m_bf16):
    hi = x.astype(BF16)
    r1 = x - hi.astype(F32)
    mid = r1.astype(BF16)
    lo = (r1 - mid.astype(F32)).astype(BF16)
    return _dot(hi, m_bf16) + _dot(mid, m_bf16) + _dot(lo, m_bf16)


def _silu(x):
    return x * jax.nn.sigmoid(x)


def _const_spec(shape):
    n = len(shape)
    return pl.BlockSpec(shape, lambda *_: (0,) * n)


def _in_proj_kernel(x_ref, g_ref, w_ref, wm_ref, wvt_ref, gq_ref, gk_ref, seg_ref,
                    cacg_ref, q_ref, k_ref, vt_ref, iq_ref, ik_ref, z_ref, xbc_ref, misc_ref):
    x = x_ref[...]
    h = x * lax.rsqrt(jnp.mean(x * x, axis=-1, keepdims=True) + EPS) * g_ref[...]
    hb = h.astype(BF16)

    def proj(c0, width):
        return _dot(hb, w_ref[:, c0:c0 + width])

    cacg_ref[...] = proj(0, 1024).astype(BF16)

    def qk_norm(c0, gain_ref, scale):
        t = proj(c0, ATT_W)
        ms = _dot((t * t).astype(BF16), seg_ref[...])
        return (t * lax.rsqrt(ms + EPS) * (gain_ref[...] * scale)).astype(BF16)

    q_ref[...] = qk_norm(1024, gq_ref, HEAD_DIM ** -0.5)
    k_ref[...] = qk_norm(1536, gk_ref, 1.0)
    vt = _dot_nt(wvt_ref[...], hb).astype(BF16)
    for j in range(vt_ref.shape[0]):
        vt_ref[j] = vt[:, j * LANES:(j + 1) * LANES]
    iq_ref[...] = proj(2048, N_IDX_HEADS * IDX_DIM).astype(BF16)
    ik_ref[...] = proj(2304, LANES).astype(BF16)
    z_ref[...] = proj(2432, SSM_INNER).astype(BF16)
    xbc_ref[...] = proj(2944, SSM_XBC).astype(BF16)
    misc_ref[...] = _dot(hb, wm_ref[...])


W_MAIN_COLS = 2944 + SSM_XBC


def _in_proj(x2, g, w_main, w_misc, w_vt, gq, gk, seg, tm):
    t, d = x2.shape
    rows = lambda w, dt: (pl.BlockSpec((tm, w), lambda i: (i, 0)), jax.ShapeDtypeStruct((t, w), dt))
    vt_out = (pl.BlockSpec((tm // LANES, ATT_W, LANES), lambda i: (i, 0, 0)),
              jax.ShapeDtypeStruct((t // LANES, ATT_W, LANES), BF16))
    outs = [rows(1024, BF16), rows(ATT_W, BF16), rows(ATT_W, BF16), vt_out,
            rows(N_IDX_HEADS * IDX_DIM, BF16), rows(LANES, BF16), rows(SSM_INNER, BF16),
            rows(SSM_XBC, BF16), rows(LANES, F32)]
    return pl.pallas_call(
        _in_proj_kernel,
        grid=(t // tm,),
        in_specs=[pl.BlockSpec((tm, d), lambda i: (i, 0)),
                  _const_spec((1, d)),
                  _const_spec((d, W_MAIN_COLS)),
                  _const_spec((d, LANES)),
                  _const_spec((ATT_W, d)),
                  _const_spec((1, ATT_W)), _const_spec((1, ATT_W)),
                  _const_spec((ATT_W, ATT_W))],
        out_specs=[spec for spec, _ in outs],
        out_shape=[shape for _, shape in outs],
        compiler_params=pltpu.CompilerParams(
            dimension_semantics=("arbitrary",), vmem_limit_bytes=VMEM_LIMIT),
        name="in_proj",
    )(x2, g, w_main, w_misc, w_vt, gq, gk, seg)


CONV_ROWS = 64


def _conv_kernel(cacg_ref, w_ref, b_ref, g_ref, beta_ref, o_ref, ext_ref):
    ts = o_ref.shape[1]
    j = pl.program_id(1)

    @pl.when(j == 0)
    def _():
        ext_ref[0:CONV_HALO, :] = jnp.zeros((CONV_HALO, CONV_CH), F32)

    @pl.when(j > 0)
    def _():
        ext_ref[0:CONV_HALO, :] = ext_ref[ts:ts + CONV_HALO, :]

    ca = cacg_ref[0, :, 0:CONV_CH].astype(F32)
    cg = cacg_ref[0, :, CONV_CH:2 * CONV_CH].astype(F32)
    ext_ref[CONV_HALO:, :] = ca * jax.nn.sigmoid(cg)

    w = w_ref[...]
    off = CONV_HALO - (CONV_K - 1)
    for r in range(ts // CONV_ROWS):
        base = r * CONV_ROWS
        acc = jnp.broadcast_to(b_ref[...], (CONV_ROWS, CONV_CH))
        for res in range(SUBLANES):
            rows = CONV_ROWS + (SUBLANES if res else 0)
            part = None
            for tap in range(CONV_K):
                if (off + tap) % SUBLANES != res:
                    continue
                start = base + off + tap - res
                term = w[tap:tap + 1, :] * ext_ref[start:start + rows, :]
                part = term if part is None else part + term
            acc = acc + part[res:res + CONV_ROWS, :]
        mu = jnp.mean(acc, axis=-1, keepdims=True)
        cen = acc - mu
        var = jnp.mean(cen * cen, axis=-1, keepdims=True)
        yn = cen * lax.rsqrt(var + EPS) * g_ref[...] + beta_ref[...]
        o_ref[0, base:base + CONV_ROWS, :] = _silu(yn).astype(BF16)


def _conv_module(cacg, w, b, g, beta, ts):
    bsz, s, _ = cacg.shape
    return pl.pallas_call(
        _conv_kernel,
        grid=(bsz, s // ts),
        in_specs=[pl.BlockSpec((1, ts, 2 * CONV_CH), lambda bi, j: (bi, j, 0)),
                  _const_spec((CONV_HALO, CONV_CH)),
                  _const_spec((1, CONV_CH)), _const_spec((1, CONV_CH)), _const_spec((1, CONV_CH))],
        out_specs=pl.BlockSpec((1, ts, CONV_CH), lambda bi, j: (bi, j, 0)),
        out_shape=jax.ShapeDtypeStruct((bsz, s, CONV_CH), BF16),
        scratch_shapes=[pltpu.VMEM((ts + CONV_HALO, CONV_CH), F32)],
        compiler_params=pltpu.CompilerParams(
            dimension_semantics=("arbitrary", "arbitrary"), vmem_limit_bytes=VMEM_LIMIT),
        name="conv_module",
    )(cacg, w, b, g, beta)


def _key_to_float(key):
    bits = key ^ ((key >> 31) & jnp.int32(0x7FFFFFFF))
    return lax.bitcast_convert_type(bits, F32)


def _float_to_key(v):
    bits = lax.bitcast_convert_type(v, jnp.int32)
    return bits ^ ((bits >> 31) & jnp.int32(0x7FFFFFFF))


KEY_NEG_INF = int(np.int32(np.uint32(0xFF800000)) ^ np.int32(0x7FFFFFFF))


def _dsa_kernel(hb_ref, q_ref, k_ref, vt_ref, iq_ref, ik_ref, misc_ref, nb_ref, o_ref,
                sc_ref, qm_ref, acc_ref, thr_ref, l_ref, *, topk):
    i = pl.program_id(1)
    s_len = k_ref.shape[1]
    q0 = i * QB
    lane = lax.broadcasted_iota(jnp.int32, (LANES, QB), 1)
    row = lax.broadcasted_iota(jnp.int32, (LANES, QB), 0)
    left = lax.broadcasted_iota(jnp.int32, (QB, LANES), 1) < HEAD_DIM
    qpos = q0 + lane

    iq = iq_ref[0]
    misc_t = misc_ref[0].T
    iqm, iw = [], []
    for h in range(N_IDX_HEADS):
        pair = iq[:, (h // 2) * LANES:(h // 2 + 1) * LANES]
        iqm.append(jnp.where(left if h % 2 == 0 else ~left, pair, jnp.zeros_like(pair)))
        iw.append(misc_t[MISC_IW + h:MISC_IW + h + 1, :] * (IDX_DIM ** -0.5 * N_IDX_HEADS ** -0.5))

    def fold(x, op):
        return op(x.reshape(LANES // SUBLANES, SUBLANES, QB), axis=0)

    def score_chunk(c, carry):
        gmax, n_ge0, n_gt0 = carry
        gmax = list(gmax)
        k0 = pl.multiple_of(c * KC, KC)
        ikc = ik_ref[0, pl.ds(k0, KC), :]
        s = jnp.zeros((KC, QB), F32)
        for h in range(N_IDX_HEADS):
            s = s + jnp.maximum(_dot_nt(ikc, iqm[h]), 0.0) * iw[h]
        for t in range(KC // LANES):
            kpos = k0 + t * LANES + row
            blk = jnp.where(kpos <= qpos, s[t * LANES:(t + 1) * LANES, :], -jnp.inf)
            sc_ref[c * (KC // LANES) + t] = blk
            g = t % (GROUPS // LANES)
            gmax[g] = jnp.maximum(gmax[g], blk)
            n_ge0 = n_ge0 + fold(jnp.where(blk >= 0.0, 1.0, 0.0), jnp.sum)
            n_gt0 = n_gt0 + fold(jnp.where(blk > 0.0, 1.0, 0.0), jnp.sum)
        return tuple(gmax), n_ge0, n_gt0

    per_chunk = KC // LANES
    n_chunks = (q0 + QB - 1) // KC + 1
    window_end = jnp.maximum(q0 - LANES, 0) + (1 + QB // LANES) * LANES
    gmax, n_ge0, n_gt0 = lax.fori_loop(
        0, (window_end + KC - 1) // KC, score_chunk,
        (tuple(jnp.full((LANES, QB), -jnp.inf, F32) for _ in range(GROUPS // LANES)),
         jnp.zeros((SUBLANES, QB), F32), jnp.zeros((SUBLANES, QB), F32)))

    def count_blocks(pred):
        def body(c, acc):
            for t in range(per_chunk):
                bk = c * per_chunk + t
                acc = acc + fold(jnp.where(pred(sc_ref[bk], bk), 1.0, 0.0), jnp.sum)
            return acc
        acc = lax.fori_loop(0, n_chunks, body, jnp.zeros((SUBLANES, QB), F32))
        return jnp.sum(acc, axis=0, keepdims=True)

    everything = jnp.float32(2 * s_len)
    vmax = jnp.max(functools.reduce(jnp.maximum, gmax), axis=0, keepdims=True)
    glow = jnp.min(functools.reduce(jnp.minimum, gmax), axis=0, keepdims=True)
    at0 = jnp.sum(n_ge0, axis=0, keepdims=True)
    above0 = jnp.sum(n_gt0, axis=0, keepdims=True)
    key_glow = _float_to_key(glow)
    positive = above0 >= topk
    negative = at0 < topk
    lo0 = jnp.where(positive, jnp.maximum(key_glow, 1), key_glow)
    c_lo0 = jnp.where(positive & (key_glow < 1), above0, everything)
    hi0 = jnp.where(negative, 0, _float_to_key(vmax) + 1)
    c_hi0 = jnp.where(negative, at0, 0.0)
    zero_tied = jnp.logical_not(positive | negative)
    closed0 = lo0 + 1 >= hi0
    done0 = zero_tied | closed0
    thr0 = jnp.where(zero_tied, 0, lo0)
    n_ge_init = jnp.where(zero_tied, at0, c_lo0)
    need0 = jnp.where(zero_tied, topk - above0, topk - c_hi0)

    steps_per_check = 4

    def search_cond(state):
        it, _, _, _, _, done, _, _, _ = state
        return (it < 36) & (jnp.min(done) == 0)

    def search_steps(state):
        return lax.fori_loop(0, steps_per_check, lambda _, st: search_step(st), state)

    def search_step(state):
        it, lo, hi, c_lo, c_hi, done, thr_key, n_ge, need = state
        mid = (lo >> 1) + (hi >> 1) + (lo & hi & 1)
        mid_b = jnp.broadcast_to(_key_to_float(mid), (LANES, QB))
        cnt = count_blocks(lambda s, bk: s >= mid_b)
        ok = cnt >= topk
        lo2, c_lo2 = jnp.where(ok, mid, lo), jnp.where(ok, cnt, c_lo)
        hi2, c_hi2 = jnp.where(ok, hi, mid), jnp.where(ok, c_hi, cnt)
        hit = cnt == topk
        newly = (done == 0) & (hit | (lo2 + 1 == hi2))
        thr_key = jnp.where(newly, jnp.where(hit, mid, lo2), thr_key)
        n_ge = jnp.where(newly, jnp.where(hit, float(topk), c_lo2), n_ge)
        need = jnp.where(newly, jnp.where(hit, everything, topk - c_hi2), need)
        frozen = done == 1
        return (it + 1, jnp.where(frozen, lo, lo2), jnp.where(frozen, hi, hi2),
                jnp.where(frozen, c_lo, c_lo2), jnp.where(frozen, c_hi, c_hi2),
                jnp.where(newly, 1, done), thr_key, n_ge, need)

    state = lax.while_loop(
        search_cond, search_steps,
        (jnp.int32(0), lo0, hi0, c_lo0, c_hi0, done0.astype(jnp.int32), thr0, n_ge_init, need0))
    thr = _key_to_float(state[6])
    n_ge, need = state[7], state[8]
    thr_b = jnp.broadcast_to(thr, (LANES, QB))
    thr_ref[...] = thr_b

    @pl.when(jnp.max(n_ge) > topk)
    def _():
        rank_mat = jnp.where(
            lax.broadcasted_iota(jnp.int32, (LANES, LANES), 0)
            >= lax.broadcasted_iota(jnp.int32, (LANES, LANES), 1), 1.0, 0.0).astype(BF16)

        def demote(c, seen):
            for t in range(per_chunk):
                bk = c * per_chunk + t
                s = sc_ref[bk]
                tie = s == thr_b
                ind = jnp.where(tie, 1.0, 0.0)
                rank = seen + _dot(rank_mat, ind.astype(BF16))
                sc_ref[bk] = jnp.where(tie & (rank > need), -jnp.inf, s)
                seen = seen + jnp.sum(ind, axis=0, keepdims=True)
            return seen

        lax.fori_loop(0, n_chunks, demote, jnp.zeros((1, QB), F32))

    q = q_ref[0]
    for pr in range(N_HEADS // 2):
        pair = q[:, pr * LANES:(pr + 1) * LANES]
        zero = jnp.zeros_like(pair)
        qm_ref[pr] = jnp.concatenate([jnp.where(left, pair, zero), jnp.where(left, zero, pair)], axis=0)
    odd =lax.broadcasted_iota(jnp.int32, (1, 2 * QB), 1) >= QB

    def mask_tile(bk, active, causal):
        t = jnp.where(active, thr_ref[...], jnp.inf)
        sel = sc_ref[bk] >= t
        if causal:
            sel = sel & (bk * LANES + row <= qpos)
        return jnp.where(sel, 0.0, NEG)

    def pair_row(values, pr):
        return jnp.where(odd, values[2 * pr + 1], values[2 * pr])

    def attend(blocks, maskadd, near, state, robust):
        width = len(blocks) * LANES
        k0 = pl.multiple_of(blocks[0] * LANES, LANES)
        maskadd2 = jnp.concatenate([maskadd, maskadd], axis=1)
        far = [hb_ref[0, h] for h in range(N_HEADS)]
        bound = [hb_ref[1, h] for h in range(N_HEADS)]
        new_state = []
        for pr in range(N_HEADS // 2):
            kp = k_ref[0, pl.ds(k0, width), pr * LANES:(pr + 1) * LANES]
            lg = _dot_nt(kp, qm_ref[pr]) + maskadd2
            if near is None:
                bias = pair_row(far, pr)
            else:
                n_w = len(blocks)
                lg = lg + jnp.concatenate(
                    [jnp.concatenate(
                        [nb_ref[2 * pr + hh, jnp.maximum(qt - w - near + n_w - 1, 0)]
                         for hh in range(2) for qt in range(QB // LANES)], axis=1)
                     for w in range(n_w)], axis=0)
                bias = 0.0
            if robust:
                m_old, l_old = state[pr]
                m_new = jnp.maximum(m_old, jnp.max(lg, axis=0, keepdims=True) + bias)
                alpha = jnp.exp(m_old - m_new)
                p = jnp.exp(lg - (m_new - bias))
            else:
                alpha = None
                p = jnp.exp(lg + (bias - pair_row(bound, pr)))
            rows = slice(pr * LANES, (pr + 1) * LANES)
            vt = jnp.concatenate([vt_ref[0, bk, rows, :] for bk in blocks], axis=1)
            lhs = jnp.concatenate([vt, jnp.ones((BF16_ROWS, width), BF16)], axis=0)
            pv = _dot(lhs, p.astype(BF16))
            colsum = pv[LANES:LANES + 1, :]
            new_state.append((m_new, alpha * l_old + colsum) if robust else state[pr] + colsum)
            for hh in range(2):
                hs = slice(pr * LANES + hh * HEAD_DIM, pr * LANES + (hh + 1) * HEAD_DIM)
                cs = slice(hh * QB, (hh + 1) * QB)
                upd = pv[hh * HEAD_DIM:(hh + 1) * HEAD_DIM, cs]
                acc_ref[hs, :] = acc_ref[hs, :] + upd if alpha is None else alpha[:, cs] * acc_ref[hs, :] + upd
        return new_state

    far_end = jnp.maximum(q0 - LANES, 0)
    window = [far_end // LANES + t for t in range(1 + QB // LANES)]
    window_shift = jnp.where(i == 0, 1, 0)

    def attention(robust):
        acc_ref[...] = jnp.zeros(acc_ref.shape, F32)
        zero = jnp.zeros((1, 2 * QB), F32)
        state = [(jnp.full((1, 2 * QB), NEG, F32), zero) if robust else zero
                 for _ in range(N_HEADS // 2)]

        def far_chunk(c, state):
            blocks = [c * per_chunk + t for t in range(per_chunk)]
            maskadd = jnp.concatenate(
                [mask_tile(bk, bk * LANES < far_end, False) for bk in blocks], axis=0)
            return attend(blocks, maskadd, None, state, robust)

        state = lax.fori_loop(0, (far_end + KC - 1) // KC, far_chunk, state)
        state = attend(window, jnp.concatenate([mask_tile(bk, True, True) for bk in window], axis=0),
                       window_shift, state, robust)
        for pr in range(N_HEADS // 2):
            l_ref[pr:pr + 1, :] = state[pr][1] if robust else state[pr]

    attention(robust=False)

    @pl.when(jnp.logical_not(jnp.min(l_ref[...]) > DENOM_FLOOR))
    def _():
        attention(robust=True)

    out_t = jnp.concatenate(
        [acc_ref[h * HEAD_DIM:(h + 1) * HEAD_DIM, :]
         / l_ref[h // 2:h // 2 + 1, (h % 2) * QB:(h % 2 + 1) * QB] for h in range(N_HEADS)],
        axis=0)
    o_ref[0] = out_t.T.astype(BF16)


def _dsa(head_scalars, q, k, vt, iq, ik2, misc, near_bias):
    bsz, s, _ = q.shape
    topk = min(TOPK_MAX, s // 4)
    nq = s // QB
    blk = lambda w: pl.BlockSpec((1, QB, w), lambda bi, i: (bi, i, 0))
    once = pl.Buffered(1)
    full = lambda w: pl.BlockSpec((1, s, w), lambda bi, i: (bi, 0, 0), pipeline_mode=once)
    return pl.pallas_call(
        functools.partial(_dsa_kernel, topk=topk),
        grid=(bsz, nq),
        in_specs=[pl.BlockSpec(memory_space=pltpu.SMEM),
                  blk(ATT_W), full(ATT_W),
                  pl.BlockSpec((1, s // LANES, ATT_W, LANES), lambda bi, i: (bi, 0, 0, 0),
                               pipeline_mode=once),
                  blk(N_IDX_HEADS * IDX_DIM), full(LANES), blk(LANES),
                  pl.BlockSpec((N_HEADS, 2 * (QB // LANES), LANES, LANES), lambda bi, i: (0, 0, 0, 0),
                               pipeline_mode=once)],
        out_specs=blk(ATT_W),
        out_shape=jax.ShapeDtypeStruct((bsz, s, ATT_W), BF16),
        scratch_shapes=[
            pltpu.VMEM((_round_up(s, KC) // LANES, LANES, QB), F32),
            pltpu.VMEM((N_HEADS // 2, 2 * QB, LANES), BF16),
            pltpu.VMEM((ATT_W, QB), F32),
            pltpu.VMEM((LANES, QB), F32),
            pltpu.VMEM((N_HEADS // 2, 2 * QB), F32),
        ],
        compiler_params=pltpu.CompilerParams(
            dimension_semantics=("arbitrary", "arbitrary"), vmem_limit_bytes=VMEM_LIMIT),
        name="dsa",
    )(head_scalars, q, k, vt, iq, ik2, misc, near_bias)


def _round_up(a, b):
    return (a + b - 1) // b * b


def _ssd_kernel(xbc_ref, z_ref, misc_ref, cw_ref, cb_ref, dtb_ref, alog_ref, dsk_ref, gn_ref,
                o_ref, ext_ref, st_ref):
    ts = o_ref.shape[1]
    j = pl.program_id(1)

    @pl.when(j == 0)
    def _():
        ext_ref[0:HALO, :] = jnp.zeros((HALO, SSM_XBC), F32)
        st_ref[...] = jnp.zeros(st_ref.shape, F32)

    @pl.when(j > 0)
    def _():
        ext_ref[0:HALO, :] = ext_ref[ts:ts + HALO, :]

    ext_ref[HALO:, :] = xbc_ref[0].astype(F32)

    lane = lax.broadcasted_iota(jnp.int32, (CHUNK, LANES), 1)
    row = lax.broadcasted_iota(jnp.int32, (CHUNK, LANES), 0)
    tril = row >= lane
    tril_b = jnp.where(tril, 1.0, 0.0).astype(BF16)
    is_dt = (lane >= MISC_DT) & (lane < MISC_DT + SSM_HEADS)
    e_row = lax.broadcasted_iota(jnp.int32, (LANES, SSM_INNER), 0)
    e_col = lax.broadcasted_iota(jnp.int32, (LANES, SSM_INNER), 1)
    expand = jnp.where(e_row - MISC_DT == e_col // SSM_HEAD_DIM, 1.0, 0.0).astype(BF16)
    left = lane < SSM_HEAD_DIM
    lane1 = lax.broadcasted_iota(jnp.int32, (1, LANES), 1)
    a_row = jnp.where((lane1 >= MISC_DT) & (lane1 < MISC_DT + SSM_HEADS), -jnp.exp(alog_ref[...]), 0.0)
    cw = cw_ref[...]
    off = HALO - (SSM_CONV_K - 1)
    gw = SSM_GROUPS * SSM_STATE

    for c in range(ts // CHUNK):
        r0 = c * CHUNK
        conv = jnp.broadcast_to(cb_ref[...], (CHUNK, SSM_XBC))
        for tap in range(SSM_CONV_K):
            conv = conv + cw[tap:tap + 1, :] * ext_ref[r0 + off + tap:r0 + off + tap + CHUNK, :]
        xc = _silu(conv)
        xs = xc[:, 0:SSM_INNER]

        dt = jnp.where(is_dt, jax.nn.softplus(misc_ref[0, r0:r0 + CHUNK, :] + dtb_ref[...]), 0.0)
        cs = _cumsum_rows(tril_b, dt * a_row)
        cs_last = cs[CHUNK - 1:CHUNK, :]
        dt_e = _dot_split(dt, expand)
        dec_e = _dot_split(jnp.exp(cs_last - cs), expand)
        ecs_e = _dot_split(jnp.exp(cs), expand)
        cs_t = cs.T

        xd = xs * dt_e
        xd_b = xd.astype(BF16)
        xdd_b = (xd * dec_e).astype(BF16)
        y = dsk_ref[...] * xs
        st_old = st_ref[...]
        st_b = st_old.astype(BF16)
        new_states = []
        y_parts = []
        for g in range(SSM_GROUPS):
            bm = xc[:, SSM_INNER + g * SSM_STATE:SSM_INNER + (g + 1) * SSM_STATE]
            cm_b = xc[:, SSM_INNER + gw + g * SSM_STATE:SSM_INNER + gw + (g + 1) * SSM_STATE].astype(BF16)
            cb = _dot_nt(cm_b, bm.astype(BF16))
            hpg = SSM_HEADS // SSM_GROUPS
            gl = slice(g * hpg * SSM_HEAD_DIM, (g + 1) * hpg * SSM_HEAD_DIM)
            y_off = _dot(cm_b, st_b[:, gl]) * ecs_e[:, gl]
            new_states.append(_dot(bm.T.astype(BF16), xdd_b[:, gl]))
            diag = []
            for pr in range(hpg // 2):
                xp = xd_b[:, gl][:, pr * LANES:(pr + 1) * LANES]
                res = []
                for hh in range(2):
                    hd = g * hpg + 2 * pr + hh
                    seg = cs[:, MISC_DT + hd:MISC_DT + hd + 1] - cs_t[MISC_DT + hd:MISC_DT + hd + 1, :]
                    decay = jnp.exp(jnp.where(tril, seg, -jnp.inf))
                    res.append(_dot((cb * decay).astype(BF16), xp))
                diag.append(jnp.where(left, res[0], res[1]))
            y_parts.append(jnp.concatenate(diag, axis=-1) + y_off)
        y = y + jnp.concatenate(y_parts, axis=-1)
        st_ref[...] = st_old * ecs_e[CHUNK - 1:CHUNK, :] + jnp.concatenate(new_states, axis=-1)

        yz = y * _silu(z_ref[0, r0:r0 + CHUNK, :].astype(F32))
        yn = yz * lax.rsqrt(jnp.mean(yz * yz, axis=-1, keepdims=True) + EPS) * gn_ref[...]
        o_ref[0, r0:r0 + CHUNK, :] = yn.astype(BF16)


def _cumsum_rows(tril_b, x):
    hi = x.astype(BF16)
    r1 = x - hi.astype(F32)
    mid = r1.astype(BF16)
    lo = (r1 - mid.astype(F32)).astype(BF16)
    return _dot(tril_b, hi) + _dot(tril_b, mid) + _dot(tril_b, lo)


def _ssd(xbc, z, misc, cw, cb, dtb, alog, dsk, gn, ts):
    bsz, s, _ = xbc.shape
    blk = lambda w: pl.BlockSpec((1, ts, w), lambda bi, j: (bi, j, 0))
    return pl.pallas_call(
        _ssd_kernel,
        grid=(bsz, s // ts),
        in_specs=[blk(SSM_XBC), blk(SSM_INNER), blk(LANES),
                  _const_spec((HALO, SSM_XBC)), _const_spec((1, SSM_XBC)),
                  _const_spec((1, LANES)), _const_spec((1, LANES)),
                  _const_spec((1, SSM_INNER)), _const_spec((1, SSM_INNER))],
        out_specs=blk(SSM_INNER),
        out_shape=jax.ShapeDtypeStruct((bsz, s, SSM_INNER), BF16),
        scratch_shapes=[pltpu.VMEM((ts + HALO, SSM_XBC), F32),
                        pltpu.VMEM((SSM_STATE, SSM_INNER), F32)],
        compiler_params=pltpu.CompilerParams(
            dimension_semantics=("arbitrary", "arbitrary"), vmem_limit_bytes=VMEM_LIMIT),
        name="ssd",
    )(xbc, z, misc, cw, cb, dtb, alog, dsk, gn)


FF_CHUNK = D_FF // 2


def _ffn_kernel(x_ref, u_ref, att_ref, y_ref, wo_ref, g_ref, wup_ref, cw_ref, cb_ref, wdn_ref,
                o_ref, ext_a, ext_g, carry_ref, gated_ref):
    tm = x_ref.shape[1]
    j = pl.program_id(1)

    @pl.when(j == 0)
    def _():
        carry_ref[...] = jnp.zeros(carry_ref.shape, F32)

    x1 = (x_ref[0]
          + _dot(u_ref[0], wo_ref[0:CONV_CH, :])
          + _dot(att_ref[0], wo_ref[CONV_CH:CONV_CH + ATT_W, :])
          + _dot(y_ref[0], wo_ref[CONV_CH + ATT_W:D_MIX, :]))
    hb = (x1 * lax.rsqrt(jnp.mean(x1 * x1, axis=-1, keepdims=True) + EPS) * g_ref[...]).astype(BF16)

    off = HALO - (FFN_CONV_K - 1)

    def conv_half(ext, c0):
        ext[0:HALO, :] = carry_ref[:, c0:c0 + FF_CHUNK]
        ext[HALO:, :] = _dot(hb, wup_ref[:, c0:c0 + FF_CHUNK])
        carry_ref[:, c0:c0 + FF_CHUNK] = ext[tm:tm + HALO, :]
        out = jnp.broadcast_to(cb_ref[:, c0:c0 + FF_CHUNK], (tm, FF_CHUNK))
        for tap in range(FFN_CONV_K):
            out = out + cw_ref[tap:tap + 1, c0:c0 + FF_CHUNK] * ext[off + tap:off + tap + tm, :]
        return out

    for c in range(D_FF // FF_CHUNK):
        fa = conv_half(ext_a, c * FF_CHUNK)
        fg = conv_half(ext_g, D_FF + c * FF_CHUNK)
        gated_ref[:, c * FF_CHUNK:(c + 1) * FF_CHUNK] = (_silu(fg) * fa).astype(BF16)
    o_ref[0] = x1 + _dot(gated_ref[...], wdn_ref[...])


def _ffn(x, u, att, y, wo, g, wup, cw, cb, wdn, tm):
    bsz, s, d = x.shape
    blk = lambda w: pl.BlockSpec((1, tm, w), lambda bi, j: (bi, j, 0))
    single = lambda shape: pl.BlockSpec(shape, lambda *_: (0,) * len(shape),
                                        pipeline_mode=pl.Buffered(1))
    return pl.pallas_call(
        _ffn_kernel,
        grid=(bsz, s // tm),
        in_specs=[blk(d), blk(CONV_CH), blk(ATT_W), blk(SSM_INNER),
                  single((D_MIX, d)), _const_spec((1, d)),
                  single((d, 2 * D_FF)), _const_spec((HALO, 2 * D_FF)), _const_spec((1, 2 * D_FF)),
                  single((D_FF, d))],
        out_specs=blk(d),
        out_shape=jax.ShapeDtypeStruct((bsz, s, d), F32),
        scratch_shapes=[pltpu.VMEM((tm + HALO, FF_CHUNK), F32),
                        pltpu.VMEM((tm + HALO, FF_CHUNK), F32),
                        pltpu.VMEM((HALO, 2 * D_FF), F32),
                        pltpu.VMEM((tm, D_FF), BF16)],
        input_output_aliases={0: 0},
        compiler_params=pltpu.CompilerParams(
            dimension_semantics=("arbitrary", "arbitrary"), vmem_limit_bytes=VMEM_LIMIT),
        name="ffn",
    )(x, u, att, y, wo, g, wup, cw, cb, wdn)


def _t5_bucket(rel):
    n = jnp.maximum(rel, 0)
    max_exact = NUM_BUCKETS // 2
    nf = jnp.maximum(n, 1).astype(F32)
    large = max_exact + (jnp.log(nf / max_exact) / math.log(MAX_DISTANCE / max_exact)
                         * (NUM_BUCKETS - max_exact)).astype(jnp.int32)
    large = jnp.minimum(large, NUM_BUCKETS - 1)
    return jnp.where(n < max_exact, n, large)


def _bias_tables(rel_bias):
    d = jnp.arange(2 * (QB // LANES))[:, None, None]
    kk = jnp.arange(LANES)[None, :, None]
    tq = jnp.arange(LANES)[None, None, :]
    rel = (d - QB // LANES + 1) * LANES + tq - kk
    onehot = (_t5_bucket(rel)[..., None] == jnp.arange(NUM_BUCKETS)).astype(F32)
    near = jnp.einsum("dkqb,bh->hdkq", onehot, rel_bias.astype(F32),
                      precision=lax.Precision.HIGHEST)
    far = rel_bias[_t5_bucket(jnp.int32(LANES + 1))].astype(F32)
    return near, far


def _pad_rows(w, rows):
    return jnp.concatenate([w, jnp.zeros((w.shape[0], rows - w.shape[1], w.shape[2]), w.dtype)], axis=1)


def _pad_lanes(v, start, width=LANES):
    out = jnp.zeros(v.shape[:-1] + (width,), v.dtype)
    return out.at[..., start:start + v.shape[-1]].set(v)


def kernel(x, rel_bias, norm_mix_g, w_in, conv_dw_w, conv_dw_b, conv_ln_g, conv_ln_b, q_norm_g, k_norm_g, ssm_conv_w, ssm_conv_b, dt_bias, a_log, d_skip, ssm_norm_g, w_out, norm_ffn_g, w_up, ffn_conv_w, ffn_conv_b, w_down):
    bsz, s, d = x.shape
    depth = w_in.shape[0]
    assert s % KC == 0 and s >= 2 * QB

    sizes = [CONV_CH, CONV_CH, ATT_W, ATT_W, ATT_W, N_IDX_HEADS * IDX_DIM, IDX_DIM, N_IDX_HEADS,
             SSM_INNER, SSM_XBC, SSM_HEADS]
    splits = [int(v) for v in np.cumsum(sizes)[:-1]]
    ca, cg, wq, wk, wv, wiq, wik, wiw, wz, wxbc, wdt = jnp.split(w_in, splits, axis=-1)
    w_main = jnp.concatenate([ca, cg, wq, wk, wiq, wik, wik, wz, wxbc], axis=-1).astype(BF16)
    w_vt = jnp.swapaxes(wv, 1, 2).astype(BF16)
    w_misc = jnp.concatenate(
        [_pad_lanes(wiw, 0, MISC_DT), _pad_lanes(wdt, 0, LANES - MISC_DT)], axis=-1).astype(BF16)
    seg = jnp.asarray(np.kron(np.eye(N_HEADS), np.full((HEAD_DIM, HEAD_DIM), 1.0 / HEAD_DIM)), BF16)

    near_bias, far_bias = _bias_tables(rel_bias)
    logit_bound = (BOUND_SLACK * HEAD_DIM ** 0.5
                   * jnp.max(jnp.abs(q_norm_g), axis=-1, keepdims=True)
                   * jnp.max(jnp.abs(k_norm_g), axis=-1, keepdims=True)
                   + jnp.max(rel_bias, axis=0)[None, :]).astype(F32)
    head_scalars = jnp.stack([jnp.broadcast_to(far_bias, logit_bound.shape), logit_bound], axis=1)

    layer_params = dict(
        g_mix=norm_mix_g[:, None, :], w_main=w_main, w_misc=w_misc, w_vt=w_vt,
        head_scalars=head_scalars,
        gq=jnp.tile(q_norm_g, (1, N_HEADS))[:, None, :], gk=jnp.tile(k_norm_g, (1, N_HEADS))[:, None, :],
        conv_w=_pad_rows(conv_dw_w, CONV_HALO), conv_b=conv_dw_b[:, None, :],
        ln_g=conv_ln_g[:, None, :], ln_b=conv_ln_b[:, None, :],
        ssm_w=_pad_rows(ssm_conv_w, HALO), ssm_b=ssm_conv_b[:, None, :],
        dtb=_pad_lanes(dt_bias, MISC_DT)[:, None, :], alog=_pad_lanes(a_log, MISC_DT)[:, None, :],
        dsk=jnp.repeat(d_skip, SSM_HEAD_DIM, axis=-1)[:, None, :], gn=ssm_norm_g[:, None, :],
        w_out=w_out.astype(BF16), g_ffn=norm_ffn_g[:, None, :], w_up=w_up.astype(BF16),
        ffn_w=_pad_rows(ffn_conv_w, HALO), ffn_b=ffn_conv_b[:, None, :], w_down=w_down.astype(BF16),
    )

    tm = min(512, s)

    def layer(xc, p):
        (cacg, q, k, vt, iq, ik2, z, xbc, misc) = _in_proj(
            xc.reshape(bsz * s, d), p["g_mix"], p["w_main"], p["w_misc"], p["w_vt"], p["gq"], p["gk"],
            seg, tm)
        r3 = lambda a: a.reshape(bsz, s, a.shape[-1])
        u = _conv_module(r3(cacg), p["conv_w"], p["conv_b"], p["ln_g"], p["ln_b"], tm)
        att = _dsa(p["head_scalars"], r3(q), r3(k), vt.reshape(bsz, s // LANES, ATT_W, LANES), r3(iq), r3(ik2),
                   r3(misc), near_bias)
        y = _ssd(r3(xbc), r3(z), r3(misc), p["ssm_w"], p["ssm_b"], p["dtb"], p["alog"],
                 p["dsk"], p["gn"], tm)
        xn = _ffn(xc, u, att, y, p["w_out"], p["g_ffn"], p["w_up"], p["ffn_w"], p["ffn_b"],
                  p["w_down"], tm)
        return xn, None

    out, _ = lax.scan(layer, x, layer_params)
    return out
```
